```python
import functools
import jax, jax.numpy as jnp
from jax import lax
import numpy as np


D_MODEL = 1024
BATCH = 4
SEQ = 8192
DEPTH = 1
DEC_BATCH = 128
DEC_SEQ = 8
PAST_LEN = 8192
PAGE_SIZE = 128

ATTN_HEADS = 8
HEAD_DIM = 64
ATTN_W = ATTN_HEADS * HEAD_DIM
CONV_CH = D_MODEL - ATTN_W
MIX_W = ATTN_W + CONV_CH
IN_W = 3 * ATTN_W + 2 * CONV_CH
CONV_TAPS = 31
MOBA_BLOCK = 256
MOBA_TOP = 3
Q_BLOCK = 128
N_EXPERTS = 32
TOP_K = 4
D_EXPERT = D_MODEL
SWIGLU_LIMIT = 7.0
SWIGLU_ALPHA = 1.702
ROW_BLOCK = 128
RMS_EPS = 1e-6
LN_EPS = 1e-5
NEG_INF = -1e30

kernel_name = "hymba_moba_conformer_moe_step"


def rms_norm(x, g):
    x32 = x.astype(jnp.float32)
    y = x32 * lax.rsqrt(jnp.mean(x32 * x32, axis=-1, keepdims=True) + RMS_EPS)
    return (y * g.astype(jnp.float32)).astype(x.dtype)


def layer_norm(x, g, b):
    x32 = x.astype(jnp.float32)
    mu = jnp.mean(x32, axis=-1, keepdims=True)
    var = jnp.mean(jnp.square(x32 - mu), axis=-1, keepdims=True)
    y = (x32 - mu) * lax.rsqrt(var + LN_EPS)
    return (y * g.astype(jnp.float32) + b.astype(jnp.float32)).astype(x.dtype)


def gathered_attend(q, k_sel, v_sel, sel_ok, k_own, v_own, own_ok):
    q32 = q.astype(jnp.float32) * (HEAD_DIM ** -0.5)
    ls = jnp.einsum('thd,thmd->thm', q32, k_sel.astype(jnp.float32))
    ls = jnp.where(sel_ok, ls, NEG_INF)
    lo = jnp.einsum('thd,hod->tho', q32, k_own.astype(jnp.float32))
    lo = jnp.where(own_ok[:, None, :], lo, NEG_INF)
    p = jax.nn.softmax(jnp.concatenate([ls, lo], axis=-1), axis=-1)
    m = ls.shape[-1]
    out = (jnp.einsum('thm,thmd->thd', p[..., :m], v_sel.astype(jnp.float32))
           + jnp.einsum('tho,hod->thd', p[..., m:], v_own.astype(jnp.float32)))
    return out.astype(q.dtype)


def moba_prompt(q, k, v):
    B, S, H, Dh = q.shape
    nb = -(-S // MOBA_BLOCK)
    pad = nb * MOBA_BLOCK - S

    def blocks(t):
        t = jnp.pad(t, ((0, 0), (0, pad), (0, 0), (0, 0)))
        return t.reshape(B, nb, MOBA_BLOCK, H, Dh).transpose(0, 1, 3, 2, 4)

    kb, vb = blocks(k), blocks(v)
    k_mean = jnp.mean(kb.astype(jnp.float32), axis=3)
    n_sel = min(MOBA_TOP, nb)
    nq = S // Q_BLOCK
    heads = jnp.arange(H)[None, :, None]

    def chunk(i):
        b = i // nq
        q0 = (i % nq) * Q_BLOCK
        qc = lax.dynamic_slice(q, (b, q0, 0, 0), (1, Q_BLOCK, H, Dh))[0]
        own = q0 // MOBA_BLOCK
        score = jnp.einsum('thd,nhd->thn', qc.astype(jnp.float32), k_mean[b])
        score = jnp.where(jnp.arange(nb) < own, score, NEG_INF)
        _, idx = lax.top_k(score, n_sel)
        k_sel = kb[b, idx, heads].reshape(Q_BLOCK, H, n_sel * MOBA_BLOCK, Dh)
        v_sel = vb[b, idx, heads].reshape(Q_BLOCK, H, n_sel * MOBA_BLOCK, Dh)
        sel_ok = jnp.repeat(jnp.arange(n_sel) < own, MOBA_BLOCK)
        qpos = q0 + jnp.arange(Q_BLOCK)
        kpos = own * MOBA_BLOCK + jnp.arange(MOBA_BLOCK)
        own_ok = kpos[None, :] <= qpos[:, None]
        return gathered_attend(qc, k_sel, v_sel, sel_ok, kb[b, own], vb[b, own], own_ok)

    out = lax.map(chunk, jnp.arange(B * nq))
    return out.reshape(B, S, H, Dh)


def moba_sample(q, k_new, v_new, cache_k, cache_v, page_table, layer):
    DB, T, H, Dh = q.shape
    n_pages = page_table.shape[1]
    past = n_pages * PAGE_SIZE
    ppb = MOBA_BLOCK // PAGE_SIZE
    nfp = past // MOBA_BLOCK
    n_sel = min(MOBA_TOP, nfp)
    tail = (past % MOBA_BLOCK) // PAGE_SIZE
    n_tail = tail * PAGE_SIZE
    heads = jnp.arange(H)[None, :, None, None]

    def one(args):
        qs, kn, vn, pt = args
        if n_sel > 0:
            rows = cache_k[layer, pt[:nfp * ppb]].astype(jnp.float32)
            k_mean = rows.reshape(nfp, ppb, H, PAGE_SIZE, Dh).mean(axis=(1, 3))
            score = jnp.einsum('thd,nhd->thn', qs.astype(jnp.float32), k_mean)
            _, idx = lax.top_k(score, n_sel)
            pid = pt[idx[..., None] * ppb + jnp.arange(ppb)]
            k_sel = cache_k[layer, pid, heads].reshape(T, H, n_sel * MOBA_BLOCK, Dh)
            v_sel = cache_v[layer, pid, heads].reshape(T, H, n_sel * MOBA_BLOCK, Dh)
        else:
            k_sel = jnp.zeros((T, H, 0, Dh), kn.dtype)
            v_sel = jnp.zeros((T, H, 0, Dh), vn.dtype)
        tail_pid = pt[n_pages - tail:]
        k_tail = cache_k[layer, tail_pid].transpose(1, 0, 2, 3).reshape(H, n_tail, Dh)
        v_tail = cache_v[layer, tail_pid].transpose(1, 0, 2, 3).reshape(H, n_tail, Dh)
        k_own = jnp.concatenate([k_tail.astype(kn.dtype), kn.transpose(1, 0, 2)], axis=1)
        v_own = jnp.concatenate([v_tail.astype(vn.dtype), vn.transpose(1, 0, 2)], axis=1)
        o = jnp.arange(n_tail + T)
        own_ok = (o[None, :] < n_tail) | (o[None, :] - n_tail <= jnp.arange(T)[:, None])
        return gathered_attend(qs, k_sel, v_sel, True, k_own, v_own, own_ok)

    return lax.map(one, (q, k_new, v_new, page_table))


def conformer_conv(a, g, prefix, conv_w, conv_b, ln_g, ln_b):
    u = a * jax.nn.sigmoid(g)
    u_ext = jnp.concatenate([prefix.astype(u.dtype), u], axis=1)
    y = lax.conv_general_dilated(u_ext, conv_w[:, None, :], window_strides=(1,), padding='VALID',
                                 dimension_numbers=('NWC', 'WIO', 'NWC'),
                                 feature_group_count=CONV_CH) + conv_b
    y = jax.nn.silu(layer_norm(y, ln_g, ln_b))
    return y, u_ext[:, -(CONV_TAPS - 1):]


def moe_ffn(h, router_w, router_b, w_gate_up, b_gate_up, w_down, b_down):
    T, D = h.shape
    logits = (h @ router_w).astype(jnp.float32) + router_b.astype(jnp.float32)
    top_logit, top_e = lax.top_k(logits, TOP_K)
    gates = jax.nn.softmax(top_logit, axis=-1)
    A = T * TOP_K
    flat_e = top_e.reshape(A)
    order = jnp.argsort(flat_e)
    sorted_e = flat_e[order]
    counts = jnp.bincount(flat_e, length=N_EXPERTS)
    padded = (counts + ROW_BLOCK - 1) // ROW_BLOCK * ROW_BLOCK
    start = jnp.cumsum(counts) - counts
    pend = jnp.cumsum(padded)
    pstart = pend - padded
    dest = pstart[sorted_e] + (jnp.arange(A) - start[sorted_e])
    n_rows = (-(-A // ROW_BLOCK) + N_EXPERTS) * ROW_BLOCK
    row_tok = jnp.zeros((n_rows,), jnp.int32).at[dest].set((order // TOP_K).astype(jnp.int32))
    row_gate = jnp.zeros((n_rows,), jnp.float32).at[dest].set(gates.reshape(A)[order])
    n_blocks = n_rows // ROW_BLOCK
    block_e = jnp.clip(jnp.searchsorted(pend, jnp.arange(n_blocks) * ROW_BLOCK, side='right'),
                       0, N_EXPERTS - 1)
    xr = h[row_tok].reshape(n_blocks, ROW_BLOCK, D)

    def expert_block(args):
        xb, e = args
        gu = xb @ w_gate_up[e] + b_gate_up[e]
        glu = jnp.minimum(gu[:, :D_EXPERT], SWIGLU_LIMIT)
        lin = jnp.clip(gu[:, D_EXPERT:], -SWIGLU_LIMIT, SWIGLU_LIMIT)
        act = glu * jax.nn.sigmoid(SWIGLU_ALPHA * glu) * (lin + 1)
        return act @ w_down[e] + b_down[e]

    yr = lax.map(expert_block, (xr, block_e)).reshape(n_rows, D)
    out = jnp.zeros((T, D), jnp.float32).at[row_tok].add(yr.astype(jnp.float32) * row_gate[:, None])
    return out.astype(h.dtype)


def trunk_layer(x, c, conv_prefix, attend, w_mod, b_mod, g_pre1, g_post1, g_pre2, g_post2, w_in,
                conv_w, conv_b, conv_ln_g, conv_ln_b, w_out, b_out, router_w, router_b,
                w_gate_up, b_gate_up, w_down, b_down):
    N, T, D = x.shape
    mod = jax.nn.silu(c) @ w_mod + b_mod
    sh1, sc1, gt1, sh2, sc2, gt2 = jnp.split(mod[:, None, :], 6, axis=-1)
    h = rms_norm(x, g_pre1) * (1 + sc1) + sh1
    proj = h @ w_in
    q, k, v, a, g = jnp.split(proj, [ATTN_W, 2 * ATTN_W, 3 * ATTN_W, 3 * ATTN_W + CONV_CH], axis=-1)
    q = q.reshape(N, T, ATTN_HEADS, HEAD_DIM)
    k = k.reshape(N, T, ATTN_HEADS, HEAD_DIM)
    v = v.reshape(N, T, ATTN_HEADS, HEAD_DIM)
    attn = attend(q, k, v).reshape(N, T, ATTN_W)
    conv, conv_state = conformer_conv(a, g, conv_prefix, conv_w, conv_b, conv_ln_g, conv_ln_b)
    mixed = jnp.concatenate([attn, conv], axis=-1) @ w_out + b_out
    x = x + gt1 * rms_norm(mixed, g_post1)
    h2 = rms_norm(x, g_pre2) * (1 + sc2) + sh2
    f = moe_ffn(h2.reshape(N * T, D), router_w, router_b, w_gate_up, b_gate_up,
                w_down, b_down).reshape(N, T, D)
    x = x + gt2 * rms_norm(f, g_post2)
    return x, k, v, conv_state


def setup_inputs(seed: int = 0) -> dict:
    key = jax.random.key(seed)
    ks = jax.random.split(key, 32)
    f32 = jnp.float32

    def nrm(k, shape, scale):
        return scale * jax.random.normal(k, shape, f32)

    n_pages = PAST_LEN // PAGE_SIZE
    n_used = DEC_BATCH * n_pages
    n_pool = n_used + n_used // 4
    page_table = jax.random.permutation(ks[5], n_pool)[:n_used].reshape(DEC_BATCH, n_pages).astype(jnp.int32)
    D = D_MODEL
    return {
        "x_prompt": nrm(ks[0], (BATCH, SEQ, D), 1.0),
        "x_sample": nrm(ks[1], (DEC_BATCH, DEC_SEQ, D), 1.0),
        "cache_k": nrm(ks[2], (DEPTH, n_pool, ATTN_HEADS, PAGE_SIZE, HEAD_DIM), 1.0),
        "cache_v": nrm(ks[3], (DEPTH, n_pool, ATTN_HEADS, PAGE_SIZE, HEAD_DIM), 1.0),
        "state_conv": nrm(ks[4], (DEPTH, DEC_BATCH, CONV_TAPS - 1, CONV_CH), 0.5),
        "page_table": page_table,
        "c_prompt": nrm(ks[6], (BATCH, D), 1.0),
        "c_sample": nrm(ks[7], (DEC_BATCH, D), 1.0),
        "w_mod": nrm(ks[8], (DEPTH, D, 6 * D), D ** -0.5),
        "b_mod": nrm(ks[9], (DEPTH, 6 * D), 0.01),
        "g_pre1": 1.0 + nrm(ks[10], (DEPTH, D), 0.05),
        "g_post1": 1.0 + nrm(ks[11], (DEPTH, D), 0.05),
        "g_pre2": 1.0 + nrm(ks[12], (DEPTH, D), 0.05),
        "g_post2": 1.0 + nrm(ks[13], (DEPTH, D), 0.05),
        "w_in": nrm(ks[14], (DEPTH, D, IN_W), D ** -0.5),
        "conv_w": nrm(ks[15], (DEPTH, CONV_TAPS, CONV_CH), CONV_TAPS ** -0.5),
        "conv_b": nrm(ks[16], (DEPTH, CONV_CH), 0.01),
        "conv_ln_g": 1.0 + nrm(ks[17], (DEPTH, CONV_CH), 0.05),
        "conv_ln_b": nrm(ks[18], (DEPTH, CONV_CH), 0.01),
        "w_out": nrm(ks[19], (DEPTH, MIX_W, D), MIX_W ** -0.5),
        "b_out": nrm(ks[20], (DEPTH, D), 0.01),
        "router_w": nrm(ks[21], (DEPTH, D, N_EXPERTS), D ** -0.5),
        "router_b": nrm(ks[22], (DEPTH, N_EXPERTS), 0.01),
        "w_gate_up": nrm(ks[23], (DEPTH, N_EXPERTS, D, 2 * D_EXPERT), D ** -0.5),
        "b_gate_up": nrm(ks[24], (DEPTH, N_EXPERTS, 2 * D_EXPERT), 0.01),
        "w_down": nrm(ks[25], (DEPTH, N_EXPERTS, D_EXPERT, D), D_EXPERT ** -0.5),
        "b_down": nrm(ks[26], (DEPTH, N_EXPERTS, D), 0.01),
    }


def reference(x_prompt, x_sample, cache_k, cache_v, state_conv, page_table, c_prompt, c_sample,
              w_mod, b_mod, g_pre1, g_post1, g_pre2, g_post2, w_in, conv_w, conv_b, conv_ln_g,
              conv_ln_b, w_out, b_out, router_w, router_b, w_gate_up, b_gate_up, w_down, b_down):
    B, S, _ = x_prompt.shape
    yp, ys = x_prompt, x_sample
    kp_l, vp_l, cp_l, ks_l, vs_l, cs_l = [], [], [], [], [], []
    for l in range(DEPTH):
        wl = (w_mod[l], b_mod[l], g_pre1[l], g_post1[l], g_pre2[l], g_post2[l], w_in[l],
              conv_w[l], conv_b[l], conv_ln_g[l], conv_ln_b[l], w_out[l], b_out[l],
              router_w[l], router_b[l], w_gate_up[l], b_gate_up[l], w_down[l], b_down[l])
        prefix_p = jnp.zeros((B, CONV_TAPS - 1, CONV_CH), x_prompt.dtype)
        yp, kp, vp, cp = trunk_layer(yp, c_prompt, prefix_p, moba_prompt, *wl)
        attend_s = functools.partial(moba_sample, cache_k=cache_k, cache_v=cache_v,
                                     page_table=page_table, layer=l)
        ys, ksn, vsn, cs = trunk_layer(ys, c_sample, state_conv[l], attend_s, *wl)
        kp_l.append(kp.reshape(B, S // PAGE_SIZE, PAGE_SIZE, ATTN_HEADS, HEAD_DIM).transpose(0, 1, 3, 2, 4))
        vp_l.append(vp.reshape(B, S // PAGE_SIZE, PAGE_SIZE, ATTN_HEADS, HEAD_DIM).transpose(0, 1, 3, 2, 4))
        cp_l.append(cp)
        ks_l.append(ksn.transpose(0, 2, 1, 3))
        vs_l.append(vsn.transpose(0, 2, 1, 3))
        cs_l.append(cs)
    return (yp, ys, jnp.stack(kp_l), jnp.stack(vp_l), jnp.stack(cp_l),
            jnp.stack(ks_l), jnp.stack(vs_l), jnp.stack(cs_l))
```

```python
import functools

import jax
import jax.numpy as jnp
from jax import lax
from jax.experimental import pallas as pl
from jax.experimental.pallas import tpu as pltpu

F32 = jnp.float32
BF16 = jnp.bfloat16
I32 = jnp.int32

ATTN_HEADS = 8
HEAD_DIM = 64
ATTN_W = ATTN_HEADS * HEAD_DIM
CONV_TAPS = 31
MOBA_BLOCK = 256
MOBA_TOP = 3
PAGE_SIZE = 128
N_EXPERTS = 32
TOP_K = 4
SWIGLU_LIMIT = 7.0
SWIGLU_ALPHA = 1.702
RMS_EPS = 1e-6
LN_EPS = 1e-5
NEG_INF = -1e30
REMOVED = -3e38

LANES = 128
SUBLANES = 8
VMEM_LIMIT = 56 * 1024 * 1024

ROW_BLOCK = 256
HALO = 32


def _cparams(sem):
    return pltpu.CompilerParams(dimension_semantics=sem, vmem_limit_bytes=VMEM_LIMIT)


def _dot(a, b):
    return jnp.dot(a, b, preferred_element_type=F32)


def _dot_nt(a, b):
    return lax.dot_general(a, b, (((1,), (1,)), ((), ())), preferred_element_type=F32)


def _split(x):
    hi = x.astype(BF16)
    lo = (x - hi.astype(F32)).astype(BF16)
    return hi, lo


def _rms(x, g):
    return x * lax.rsqrt(jnp.mean(x * x, axis=-1, keepdims=True) + RMS_EPS) * g


def _top_n(score, lane_f, n):
    out = []
    work = score
    for _ in range(n):
        m = jnp.max(work, axis=1, keepdims=True)
        idx = jnp.min(jnp.where(work == m, lane_f, 1e9), axis=1, keepdims=True)
        out.append((m, idx))
        work = jnp.where(lane_f == idx, REMOVED, work)
    return out


def _mod_kernel(c_ref, w_ref, b_ref, o_ref):
    c = c_ref[...]
    s = (c * jax.nn.sigmoid(c)).astype(BF16)
    o_ref[...] = _dot(s, w_ref[...].astype(BF16)) + b_ref[...]


def _mod(c, w, b):
    n, d = c.shape
    dout = w.shape[1]
    bn = 1536
    return pl.pallas_call(
        _mod_kernel,
        grid=(dout // bn,),
        in_specs=[pl.BlockSpec((n, d), lambda j: (0, 0)),
                  pl.BlockSpec((d, bn), lambda j: (0, j)),
                  pl.BlockSpec((1, bn), lambda j: (0, j))],
        out_specs=pl.BlockSpec((n, bn), lambda j: (0, j)),
        out_shape=jax.ShapeDtypeStruct((n, dout), F32),
        compiler_params=_cparams(("arbitrary",)),
        name="mod",
    )(c, w, b.reshape(1, dout))


def _in_proj_kernel(x_ref, sh_ref, sc_ref, g_ref, w_ref, *outs, prompt):
    x = x_ref[0]
    tm = x.shape[0]
    h = _rms(x, g_ref[...]) * (1.0 + sc_ref[0]) + sh_ref[0]
    proj = _dot(h.astype(BF16), w_ref[...])
    q = proj[:, :ATTN_W] * (HEAD_DIM ** -0.5)
    k = proj[:, ATTN_W:2 * ATTN_W]
    v = proj[:, 2 * ATTN_W:3 * ATTN_W]
    c = (proj.shape[1] - 3 * ATTN_W) // 2
    a = proj[:, 3 * ATTN_W:3 * ATTN_W + c]
    g = proj[:, 3 * ATTN_W + c:]
    u = a * jax.nn.sigmoid(g)
    if prompt:
        q_ref, kb_ref, vb_ref, kf_ref, vf_ref, u_ref, km_ref = outs
        q_ref[0] = q.astype(BF16)
        kb_ref[0] = k.astype(BF16)
        vb_ref[0] = v.astype(BF16)
        u_ref[0] = u
        for p in range(tm // PAGE_SIZE):
            rows = slice(p * PAGE_SIZE, (p + 1) * PAGE_SIZE)
            for hh in range(ATTN_HEADS):
                cols = slice(hh * HEAD_DIM, (hh + 1) * HEAD_DIM)
                kf_ref[0, p, hh] = k[rows, cols]
                vf_ref[0, p, hh] = v[rows, cols]
        for blk in range(tm // MOBA_BLOCK):
            rows = slice(blk * MOBA_BLOCK, (blk + 1) * MOBA_BLOCK)
            km_ref[0, blk] = jnp.mean(k[rows], axis=0, keepdims=True)
    else:
        q_ref, kf_ref, vf_ref, u_ref = outs
        u_ref[0] = u
        t = q_ref.shape[2]
        for hh in range(ATTN_HEADS):
            cols = slice(hh * HEAD_DIM, (hh + 1) * HEAD_DIM)
            q_ref[:, hh] = q[:, cols].reshape(tm // t, t, HEAD_DIM)
            kf_ref[:, hh] = k[:, cols].reshape(tm // t, t, HEAD_DIM)
            vf_ref[:, hh] = v[:, cols].reshape(tm // t, t, HEAD_DIM)


def _mod_spec(per_row, tm, d, chunk):
    if per_row:
        return pl.BlockSpec((1, tm, d), lambda b, i: (0, i, chunk))
    return pl.BlockSpec((1, 1, d), lambda b, i: (b, 0, chunk))


def _in_proj_prompt(x, mod, g_pre1, w_in_bf):
    bsz, s, d = x.shape
    tm = 512
    c = (w_in_bf.shape[1] - 3 * ATTN_W) // 2
    n_pages = s // PAGE_SIZE
    nb = s // MOBA_BLOCK
    row = lambda w: pl.BlockSpec((1, tm, w), lambda b, i: (b, i, 0))
    paged = pl.BlockSpec((1, tm // PAGE_SIZE, ATTN_HEADS, PAGE_SIZE, HEAD_DIM), lambda b, i: (b, i, 0, 0, 0))
    return pl.pallas_call(
        functools.partial(_in_proj_kernel, prompt=True),
        grid=(bsz, s // tm),
        in_specs=[row(d), _mod_spec(False, tm, d, 0), _mod_spec(False, tm, d, 1),
                  pl.BlockSpec((1, d), lambda b, i: (0, 0)),
                  pl.BlockSpec(w_in_bf.shape, lambda b, i: (0, 0))],
        out_specs=[row(ATTN_W), row(ATTN_W), row(ATTN_W), paged, paged, row(c),
                   pl.BlockSpec((1, tm // MOBA_BLOCK, 1, ATTN_W), lambda b, i: (b, i, 0, 0))],
        out_shape=[jax.ShapeDtypeStruct((bsz, s, ATTN_W), BF16)] * 3
        + [jax.ShapeDtypeStruct((bsz, n_pages, ATTN_HEADS, PAGE_SIZE, HEAD_DIM), F32)] * 2
        + [jax.ShapeDtypeStruct((bsz, s, c), F32),
           jax.ShapeDtypeStruct((bsz, nb, 1, ATTN_W), F32)],
        compiler_params=_cparams(("arbitrary", "arbitrary")),
        name="in_proj_prompt",
    )(x, mod, mod, g_pre1, w_in_bf)


def _in_proj_sample(x, mod_rows, g_pre1, w_in_bf):
    db, t, d = x.shape
    rows = db * t
    tm = 128
    c = (w_in_bf.shape[1] - 3 * ATTN_W) // 2
    heads = pl.BlockSpec((tm // t, ATTN_HEADS, t, HEAD_DIM), lambda b, i: (i, 0, 0, 0))
    return pl.pallas_call(
        functools.partial(_in_proj_kernel, prompt=False),
        grid=(1, rows // tm),
        in_specs=[pl.BlockSpec((1, tm, d), lambda b, i: (0, i, 0)),
                  _mod_spec(True, tm, d, 0), _mod_spec(True, tm, d, 1),
                  pl.BlockSpec((1, d), lambda b, i: (0, 0)),
                  pl.BlockSpec(w_in_bf.shape, lambda b, i: (0, 0))],
        out_specs=[heads, heads, heads, pl.BlockSpec((1, tm, c), lambda b, i: (0, i, 0))],
        out_shape=[jax.ShapeDtypeStruct((db, ATTN_HEADS, t, HEAD_DIM), F32)] * 3
        + [jax.ShapeDtypeStruct((1, rows, c), F32)],
        compiler_params=_cparams(("arbitrary", "arbitrary")),
        name="in_proj_sample",
    )(x.reshape(1, rows, d), mod_rows, mod_rows, g_pre1, w_in_bf)


def _moba_prompt_kernel(q_ref, k_ref, v_ref, km_ref, o_ref):
    s_len = q_ref.shape[1]
    nb = s_len // MOBA_BLOCK
    lane = lax.broadcasted_iota(I32, (1, LANES), 1)
    lane_f = lane.astype(F32)
    row_i = lax.broadcasted_iota(I32, (MOBA_BLOCK, MOBA_BLOCK), 0)
    col_i = lax.broadcasted_iota(I32, (MOBA_BLOCK, MOBA_BLOCK), 1)
    causal = col_i <= row_i
    km = km_ref[0, :, 0, :]

    heads = []
    for hh in range(2):
        in_head = (lane >= hh * HEAD_DIM) & (lane < (hh + 1) * HEAD_DIM)
        off = HEAD_DIM * (1 - hh)
        km_h = jnp.where(in_head, km, 0.0)
        parts = []
        if off:
            parts.append(jnp.zeros((off, LANES), F32))
        parts.append(km_h)
        parts.append(jnp.zeros((LANES - off - nb, LANES), F32))
        km_hi, km_lo = _split(jnp.concatenate(parts, axis=0))
        heads.append((in_head, off, km_hi, km_lo))

    def q_tile(i, carry):
        q0 = pl.multiple_of(i * MOBA_BLOCK, MOBA_BLOCK)
        q = q_ref[0, pl.ds(q0, MOBA_BLOCK), :]
        kd = k_ref[0, pl.ds(q0, MOBA_BLOCK), :]
        vd = v_ref[0, pl.ds(q0, MOBA_BLOCK), :]
        outs = []
        for in_head, off, km_hi, km_lo in heads:
            qh = jnp.where(in_head, q, jnp.zeros_like(q))
            blk = lane - off
            in_bias = (blk >= 0) & (blk < nb)
            valid = in_bias & (blk < i)
            score = jnp.where(valid, _dot_nt(qh, km_hi) + _dot_nt(qh, km_lo), NEG_INF)
            sel = jnp.zeros(score.shape, F32)
            for _, idx in _top_n(score, lane_f, MOBA_TOP):
                sel = jnp.where((lane_f == idx) & valid, 1.0, sel)
            bias = jnp.where(in_bias & (sel == 0.0), NEG_INF, 0.0).astype(BF16)
            q_aug = jnp.where(in_head, q, bias)

            s = jnp.where(causal, _dot_nt(qh, kd), NEG_INF)
            m = jnp.max(s, axis=1, keepdims=True)
            p = jnp.exp(s - m)
            acc = _dot(p.astype(BF16), jnp.where(in_head, vd, jnp.ones_like(vd)))

            def kv_step(j, mc, in_head=in_head, off=off, q_aug=q_aug):
                m, acc = mc
                k0 = pl.multiple_of(j * MOBA_BLOCK, MOBA_BLOCK)
                kj = k_ref[0, pl.ds(k0, MOBA_BLOCK), :]
                vj = v_ref[0, pl.ds(k0, MOBA_BLOCK), :]
                onehot = jnp.where(lane == off + j, 1.0, 0.0).astype(BF16)
                kh = jnp.where(in_head, kj, onehot)
                vh = jnp.where(in_head, vj, jnp.ones_like(vj))
                s = _dot_nt(q_aug, kh)
                m_new = jnp.maximum(m, jnp.max(s, axis=1, keepdims=True))
                alpha = jnp.exp(m - m_new)
                p = jnp.exp(s - m_new)
                return m_new, alpha * acc + _dot(p.astype(BF16), vh)

            m, acc = lax.fori_loop(0, i, kv_step, (m, acc))
            denom = pltpu.roll(acc, HEAD_DIM, axis=1)
            outs.append(acc / denom)
        o_ref[0, pl.ds(q0, MOBA_BLOCK), :] = jnp.where(heads[0][0], outs[0], outs[1]).astype(o_ref.dtype)
        return carry

    lax.fori_loop(0, nb, q_tile, 0)


def _moba_prompt(q, k, v, km):
    bsz, s, _ = q.shape
    assert 2 * HEAD_DIM == LANES and s % MOBA_BLOCK == 0 and s // MOBA_BLOCK <= HEAD_DIM
    nb = s // MOBA_BLOCK
    spec = pl.BlockSpec((1, s, LANES), lambda b, hp: (b, 0, hp))
    return pl.pallas_call(
        _moba_prompt_kernel,
        grid=(bsz, ATTN_W // LANES),
        in_specs=[spec, spec, spec, pl.BlockSpec((1, nb, 1, LANES), lambda b, hp: (b, 0, 0, hp))],
        out_specs=spec,
        out_shape=jax.ShapeDtypeStruct((bsz, s, ATTN_W), BF16),
        compiler_params=_cparams(("arbitrary", "arbitrary")),
        name="moba_prompt",
    )(q, k, v, km)


def _moba_sample_kernel(pt_ref, q_ref, kn_ref, vn_ref, ck_ref, cv_ref, o_ref, kbuf, vbuf, sems):
    n_pages = kbuf.shape[1]
    n_heads = pl.num_programs(1)
    seq = pl.program_id(0)
    head = pl.program_id(1)
    step = seq * n_heads + head
    total = pl.num_programs(0) * n_heads
    slot = step % 2

    def page_copies(sq, hd, sl, p):
        page = pt_ref[sq * n_pages + p]
        return (pltpu.make_async_copy(ck_ref.at[page, hd], kbuf.at[sl, p], sems.at[0, sl]),
                pltpu.make_async_copy(cv_ref.at[page, hd], vbuf.at[sl, p], sems.at[1, sl]))

    def fetch(sq, hd, sl):
        def body(p, c):
            ck, cv = page_copies(sq, hd, sl, p)
            ck.start()
            cv.start()
            return c
        lax.fori_loop(0, n_pages, body, 0)

    @pl.when(step == 0)
    def _():
        fetch(seq, head, slot)

    @pl.when(step + 1 < total)
    def _():
        nxt = step + 1
        fetch(nxt // n_heads, nxt % n_heads, 1 - slot)

    def drain(p, c):
        ck, cv = page_copies(seq, head, slot, p)
        ck.wait()
        cv.wait()
        return c
    lax.fori_loop(0, n_pages, drain, 0)

    past = n_pages * PAGE_SIZE
    nfp = past // MOBA_BLOCK
    t = q_ref.shape[2]
    kp = kbuf[slot].reshape(past, HEAD_DIM)
    vp = vbuf[slot].reshape(past, HEAD_DIM)
    q = q_ref[0, 0]
    kn = kn_ref[0, 0]
    vn = vn_ref[0, 0]

    km = jnp.mean(kp.reshape(nfp, MOBA_BLOCK, HEAD_DIM), axis=1)
    km = jnp.concatenate([km, jnp.zeros((LANES - nfp, HEAD_DIM), F32)], axis=0)
    q_hi, q_lo = _split(q)
    km_hi, km_lo = _split(km)
    score = _dot_nt(q_hi, km_hi) + _dot_nt(q_hi, km_lo) + _dot_nt(q_lo, km_hi)
    lane_f = lax.broadcasted_iota(I32, (1, LANES), 1).astype(F32)
    score = jnp.where(lane_f < nfp, score, REMOVED)
    picks = _top_n(score, lane_f, min(MOBA_TOP, nfp))

    qb = q.astype(BF16)
    s = _dot_nt(qb, kp.astype(BF16))
    key_blk = (lax.broadcasted_iota(I32, (1, past), 1) // MOBA_BLOCK).astype(F32)
    keep = key_blk == picks[0][1]
    for _, idx in picks[1:]:
        keep = keep | (key_blk == idx)
    s = jnp.where(keep, s, NEG_INF)
    so = _dot_nt(qb, kn.astype(BF16))
    causal = lax.broadcasted_iota(I32, (t, t), 1) <= lax.broadcasted_iota(I32, (t, t), 0)
    so = jnp.where(causal, so, NEG_INF)
    m = jnp.maximum(jnp.max(s, axis=1, keepdims=True), jnp.max(so, axis=1, keepdims=True))
    p = jnp.exp(s - m)
    po = jnp.exp(so - m)
    denom = jnp.sum(p, axis=1, keepdims=True) + jnp.sum(po, axis=1, keepdims=True)
    out = _dot(p.astype(BF16), vp.astype(BF16)) + _dot(po.astype(BF16), vn.astype(BF16))
    o_ref[0, 0] = out / denom


def _moba_sample(q, kn, vn, cache_k, cache_v, page_table):
    db, nh, t, _ = q.shape
    n_pages = page_table.shape[1]
    assert (n_pages * PAGE_SIZE) % MOBA_BLOCK == 0 and n_pages * PAGE_SIZE >= MOBA_BLOCK
    new = pl.BlockSpec((1, 1, t, HEAD_DIM), lambda s, h, pt: (s, h, 0, 0))
    return pl.pallas_call(
        _moba_sample_kernel,
        grid_spec=pltpu.PrefetchScalarGridSpec(
            num_scalar_prefetch=1,
            grid=(db, nh),
            in_specs=[new, new, new, pl.BlockSpec(memory_space=pl.ANY), pl.BlockSpec(memory_space=pl.ANY)],
            out_specs=new,
            scratch_shapes=[pltpu.VMEM((2, n_pages, PAGE_SIZE, HEAD_DIM), F32),
                            pltpu.VMEM((2, n_pages, PAGE_SIZE, HEAD_DIM), F32),
                            pltpu.SemaphoreType.DMA((2, 2))]),
        out_shape=jax.ShapeDtypeStruct((db, nh, t, HEAD_DIM), F32),
        compiler_params=_cparams(("arbitrary", "arbitrary")),
        name="moba_sample",
    )(page_table.reshape(-1), q, kn, vn, cache_k, cache_v)


def _conv_tail(y, lng_ref, lnb_ref):
    mu = jnp.mean(y, axis=-1, keepdims=True)
    yc = y - mu
    var = jnp.mean(yc * yc, axis=-1, keepdims=True)
    z = yc * lax.rsqrt(var + LN_EPS) * lng_ref[...] + lnb_ref[...]
    return z * jax.nn.sigmoid(z)


def _post_mix_tail(attn, conv, x, gt1, sh2, sc2, wo_ref, bo_ref, gp1_ref, gp2_ref, rw_ref, rb_ref,
                   cnt_ref, x1_ref, h2_ref, meta_ref, gate_ref):
    tm = x.shape[0]
    half = attn.shape[1]
    mixed = _dot(attn, wo_ref[:half, :]) + _dot(conv.astype(BF16), wo_ref[half:, :]) + bo_ref[...]
    x1 = x + gt1 * _rms(mixed, gp1_ref[...])
    h2 = _rms(x1, gp2_ref[...]) * (1.0 + sc2) + sh2
    x1_ref[...] = x1.reshape(x1_ref.shape)
    h2_ref[...] = h2.reshape(h2_ref.shape)

    lane = lax.broadcasted_iota(I32, (1, LANES), 1)
    lane_f = lane.astype(F32)
    h_hi, h_lo = _split(h2)
    w_hi, w_lo = _split(rw_ref[...])
    logits = _dot(h_hi, w_hi) + _dot(h_hi, w_lo) + _dot(h_lo, w_hi) + rb_ref[...]
    logits = jnp.where(lane < N_EXPERTS, logits, REMOVED)
    picks = _top_n(logits, lane_f, TOP_K)
    exps = [jnp.exp(val - picks[0][0]) for val, _ in picks]
    total = exps[0]
    for e in exps[1:]:
        total = total + e
    sel = jnp.zeros((tm, LANES), F32)
    for _, idx in picks:
        sel = jnp.where(lane_f == idx, 1.0, sel)

    before = lax.broadcasted_iota(I32, (tm, tm), 1) < lax.broadcasted_iota(I32, (tm, tm), 0)
    rank_all = cnt_ref[...] + _dot(jnp.where(before, 1.0, 0.0).astype(BF16), sel.astype(BF16))
    cnt_ref[...] = cnt_ref[...] + jnp.sum(sel, axis=0, keepdims=True)

    meta = jnp.zeros((tm, LANES), F32)
    gates = jnp.zeros((tm, LANES), F32)
    for kk, (_, idx) in enumerate(picks):
        rank = jnp.sum(jnp.where(lane_f == idx, rank_all, 0.0), axis=1, keepdims=True)
        meta = jnp.where(lane == kk, idx, meta)
        meta = jnp.where(lane == TOP_K + kk, rank, meta)
        gates = jnp.where(lane == kk, exps[kk] / total, gates)
    meta_ref[...] = meta.astype(I32).reshape(meta_ref.shape)
    gate_ref[...] = gates.reshape(gate_ref.shape)


def _post_mix_prompt_kernel(attn_ref, u_ref, halo_ref, x_ref, gt1_ref, sh2_ref, sc2_ref, cw_ref, cb_ref,
                            lng_ref, lnb_ref, wo_ref, bo_ref, gp1_ref, gp2_ref, rw_ref, rb_ref, cnt0_ref,
                            x1_ref, h2_ref, meta_ref, gate_ref, cnt_ref, ext):
    b = pl.program_id(0)
    i = pl.program_id(1)
    tm = u_ref.shape[1]

    @pl.when((b == 0) & (i == 0))
    def _():
        cnt_ref[...] = cnt0_ref[...]

    ext[0:HALO, :] = jnp.where(i > 0, halo_ref[0], 0.0)
    ext[HALO:HALO + tm, :] = u_ref[0]
    first = HALO - (CONV_TAPS - 1)
    y = cb_ref[...] + cw_ref[0:1, :] * ext[first:first + tm, :]
    for j in range(1, CONV_TAPS):
        y = y + cw_ref[j:j + 1, :] * ext[first + j:first + j + tm, :]
    conv = _conv_tail(y, lng_ref, lnb_ref)
    _post_mix_tail(attn_ref[0], conv, x_ref[0], gt1_ref[0], sh2_ref[0], sc2_ref[0], wo_ref, bo_ref,
                   gp1_ref, gp2_ref, rw_ref, rb_ref, cnt_ref, x1_ref, h2_ref, meta_ref, gate_ref)


def _post_mix_sample_kernel(attn_ref, u_ref, st_ref, x_ref, gt1_ref, sh2_ref, sc2_ref, cw_ref, cb_ref,
                            lng_ref, lnb_ref, wo_ref, bo_ref, gp1_ref, gp2_ref, rw_ref, rb_ref, cnt0_ref,
                            x1_ref, h2_ref, meta_ref, gate_ref, cnt_ref, st_out_ref, ext):
    i = pl.program_id(1)
    nseq, keep, c = st_ref.shape
    t = u_ref.shape[1] // nseq

    @pl.when(i == 0)
    def _():
        cnt_ref[...] = cnt0_ref[...]

    ext[:, 0:keep, :] = st_ref[...]
    ext[:, keep:keep + t, :] = u_ref[0].reshape(nseq, t, c)
    y = cb_ref[...] + cw_ref[0:1, :] * ext[:, 0:t, :]
    for j in range(1, CONV_TAPS):
        y = y + cw_ref[j:j + 1, :] * ext[:, j:j + t, :]
    st_out_ref[...] = ext[:, t:t + keep, :]
    conv = _conv_tail(y.reshape(nseq * t, c), lng_ref, lnb_ref)
    _post_mix_tail(attn_ref[0], conv, x_ref[0], gt1_ref[0], sh2_ref[0], sc2_ref[0], wo_ref, bo_ref,
                   gp1_ref, gp2_ref, rw_ref, rb_ref, cnt_ref, x1_ref, h2_ref, meta_ref, gate_ref)


def _post_mix(prompt, attn, u, prefix, x, mod, cnt0, cw, cb, lng, lnb, wo_bf, bo, gp1, gp2, rw, rb):
    bsz, s, d = x.shape
    c = u.shape[2]
    tm = 256 if prompt else 128
    row = lambda w: pl.BlockSpec((1, tm, w), lambda b, i: (b, i, 0))
    full = lambda a: pl.BlockSpec(a.shape, lambda b, i: (0,) * a.ndim)
    weights = (cw, cb, lng, lnb, wo_bf, bo, gp1, gp2, rw, rb, cnt0)
    mods = [_mod_spec(not prompt, tm, d, ch) for ch in (2, 3, 4)]
    out_specs = [row(d), row(d), row(LANES), row(LANES), pl.BlockSpec((1, LANES), lambda b, i: (0, 0))]
    out_shape = [jax.ShapeDtypeStruct((bsz, s, d), F32), jax.ShapeDtypeStruct((bsz, s, d), F32),
                 jax.ShapeDtypeStruct((bsz, s, LANES), I32), jax.ShapeDtypeStruct((bsz, s, LANES), F32),
                 jax.ShapeDtypeStruct((1, LANES), F32)]
    if prompt:
        kern = _post_mix_prompt_kernel
        per = tm // HALO
        side = pl.BlockSpec((1, HALO, c), lambda b, i: (b, jnp.maximum(i * per - 1, 0), 0))
        side_arg = u
        scratch = [pltpu.VMEM((HALO + tm, c), F32)]
    else:
        kern = _post_mix_sample_kernel
        nseq, keep = prefix.shape[0], prefix.shape[1]
        t = s // nseq
        per = tm // t
        side = pl.BlockSpec((per, keep, c), lambda b, i: (i, 0, 0))
        side_arg = prefix
        out_specs.append(pl.BlockSpec((per, keep, c), lambda b, i: (i, 0, 0)))
        out_shape.append(jax.ShapeDtypeStruct(prefix.shape, F32))
        scratch = [pltpu.VMEM((per, -(-(keep + t) // SUBLANES) * SUBLANES, c), F32)]
    return pl.pallas_call(
        kern,
        grid=(bsz, s // tm),
        in_specs=[row(ATTN_W), row(c), side, row(d)] + mods + [full(a) for a in weights],
        out_specs=out_specs,
        out_shape=out_shape,
        scratch_shapes=scratch,
        compiler_params=_cparams(("arbitrary", "arbitrary")),
        name="post_mix_prompt" if prompt else "post_mix_sample",
    )(attn, u, side_arg, x, mod, mod, mod, *weights)


def _moe_gather_kernel(tok_ref, h_ref, o_ref, sem):
    rb = o_ref.shape[0]

    def copy(r):
        return pltpu.make_async_copy(h_ref.at[pl.ds(tok_ref[0, 0, r], 1), :], o_ref.at[pl.ds(r, 1), :], sem)

    def start(r, c):
        copy(r).start()
        return c

    def wait(r, c):
        copy(r).wait()
        return c

    lax.fori_loop(0, rb, start, 0)
    lax.fori_loop(0, rb, wait, 0)


def _moe_gather(row_tok, h):
    n_rows = row_tok.shape[0]
    d = h.shape[1]
    nblk = n_rows // ROW_BLOCK
    return pl.pallas_call(
        _moe_gather_kernel,
        grid=(nblk,),
        in_specs=[pl.BlockSpec((1, 1, ROW_BLOCK), lambda b: (b, 0, 0), memory_space=pltpu.SMEM),
                  pl.BlockSpec(memory_space=pl.ANY)],
        out_specs=pl.BlockSpec((ROW_BLOCK, d), lambda b: (b, 0)),
        out_shape=jax.ShapeDtypeStruct((n_rows, d), F32),
        scratch_shapes=[pltpu.SemaphoreType.DMA(())],
        compiler_params=_cparams(("arbitrary",)),
        name="moe_gather",
    )(row_tok.reshape(nblk, 1, ROW_BLOCK), h)


def _moe_expert_kernel(be_ref, x_ref, wgu_ref, bgu_ref, wd_ref, bd_ref, y_ref, wgu_bf, wd_bf):
    b = pl.program_id(0)
    prev = be_ref[jnp.maximum(b - 1, 0)]

    @pl.when((b == 0) | (be_ref[b] != prev))
    def _():
        wgu_bf[...] = wgu_ref[0].astype(BF16)
        wd_bf[...] = wd_ref[0].astype(BF16)

    de = wd_bf.shape[0]
    gu = _dot(x_ref[...].astype(BF16), wgu_bf[...]) + bgu_ref[0]
    glu = jnp.minimum(gu[:, :de], SWIGLU_LIMIT)
    lin = jnp.clip(gu[:, de:], -SWIGLU_LIMIT, SWIGLU_LIMIT)
    act = glu * jax.nn.sigmoid(SWIGLU_ALPHA * glu) * (lin + 1.0)
    y_ref[...] = _dot(act.astype(BF16), wd_bf[...]) + bd_ref[0]


def _moe_expert(block_e, xr, wgu, bgu, wd, bd):
    n_rows, d = xr.shape
    ne, _, two_de = wgu.shape
    de = wd.shape[1]
    nblk = n_rows // ROW_BLOCK
    return pl.pallas_call(
        _moe_expert_kernel,
        grid_spec=pltpu.PrefetchScalarGridSpec(
            num_scalar_prefetch=1,
            grid=(nblk,),
            in_specs=[pl.BlockSpec((ROW_BLOCK, d), lambda b, be: (b, 0)),
                      pl.BlockSpec((1, d, two_de), lambda b, be: (be[b], 0, 0)),
                      pl.BlockSpec((1, 1, two_de), lambda b, be: (be[b], 0, 0)),
                      pl.BlockSpec((1, de, d), lambda b, be: (be[b], 0, 0)),
                      pl.BlockSpec((1, 1, d), lambda b, be: (be[b], 0, 0))],
            out_specs=pl.BlockSpec((ROW_BLOCK, d), lambda b, be: (b, 0)),
            scratch_shapes=[pltpu.VMEM((d, two_de), BF16), pltpu.VMEM((de, d), BF16)]),
        out_shape=jax.ShapeDtypeStruct((n_rows, d), F32),
        compiler_params=_cparams(("arbitrary",)),
        name="moe_expert",
    )(block_e, xr, wgu, bgu.reshape(ne, 1, two_de), wd, bd.reshape(ne, 1, d))


def _moe_combine_kernel(pos_ref, yr_ref, gate_ref, x1_ref, gt2_ref, g_ref, o_ref, buf, sem):
    tm = buf.shape[1]

    def copy(r, kk):
        src = yr_ref.at[pl.ds(pos_ref[0, 0, r * TOP_K + kk], 1), :]
        return pltpu.make_async_copy(src, buf.at[kk, pl.ds(r, 1), :], sem)

    def start(r, c):
        for kk in range(TOP_K):
            copy(r, kk).start()
        return c

    def wait(r, c):
        for kk in range(TOP_K):
            copy(r, kk).wait()
        return c

    lax.fori_loop(0, tm, start, 0)
    lax.fori_loop(0, tm, wait, 0)
    gates = gate_ref[0]
    f = gates[:, 0:1] * buf[0]
    for kk in range(1, TOP_K):
        f = f + gates[:, kk:kk + 1] * buf[kk]
    o_ref[0] = x1_ref[0] + gt2_ref[0] * _rms(f, g_ref[...])


def _moe_combine(prompt, pos, tile0, yr, gates, x1, mod, g_post2):
    bsz, s, d = x1.shape
    tm = 256
    per_b = s // tm
    row = lambda w: pl.BlockSpec((1, tm, w), lambda b, i: (b, i, 0))
    return pl.pallas_call(
        _moe_combine_kernel,
        grid=(bsz, per_b),
        in_specs=[pl.BlockSpec((1, 1, tm * TOP_K), lambda b, i: (tile0 + b * per_b + i, 0, 0),
                               memory_space=pltpu.SMEM),
                  pl.BlockSpec(memory_space=pl.ANY),
                  row(LANES), row(d), _mod_spec(not prompt, tm, d, 5),
                  pl.BlockSpec((1, d), lambda b, i: (0, 0))],
        out_specs=row(d),
        out_shape=jax.ShapeDtypeStruct((bsz, s, d), F32),
        scratch_shapes=[pltpu.VMEM((TOP_K, tm, d), F32), pltpu.SemaphoreType.DMA(())],
        compiler_params=_cparams(("arbitrary", "arbitrary")),
        name="moe_combine_prompt" if prompt else "moe_combine_sample",
    )(pos, yr, gates, x1, mod, g_post2)


def kernel(x_prompt, x_sample, cache_k, cache_v, state_conv, page_table, c_prompt, c_sample, w_mod, b_mod,
           g_pre1, g_post1, g_pre2, g_post2, w_in, conv_w, conv_b, conv_ln_g, conv_ln_b, w_out, b_out,
           router_w, router_b, w_gate_up, b_gate_up, w_down, b_down):
    assert w_mod.shape[0] == 1, "single-layer trunk"
    bsz, s, d = x_prompt.shape
    db, t, _ = x_sample.shape
    c = conv_w.shape[2]
    keep = CONV_TAPS - 1
    rows_s = db * t

    mod = _mod(jnp.concatenate([c_prompt, c_sample], axis=0), w_mod[0], b_mod[0])
    mod_p = mod[:bsz].reshape(bsz, 1, 6 * d)
    mod_s = jnp.repeat(mod[bsz:], t, axis=0).reshape(1, rows_s, 6 * d)

    w_in_bf = w_in[0].astype(BF16)
    wo_bf = w_out[0].astype(BF16)
    cw = jnp.concatenate([conv_w[0], jnp.zeros((HALO - CONV_TAPS, c), F32)], axis=0)
    rw = jnp.concatenate([router_w[0], jnp.zeros((d, LANES - N_EXPERTS), F32)], axis=1)
    rb = jnp.concatenate([router_b[0], jnp.zeros((LANES - N_EXPERTS,), F32)]).reshape(1, LANES)
    shared = (cw, conv_b, conv_ln_g, conv_ln_b, wo_bf, b_out, g_post1, g_pre2, rw, rb)

    q_p, kb_p, vb_p, k_pages, v_pages, u_p, km_p = _in_proj_prompt(x_prompt, mod_p, g_pre1, w_in_bf)
    q_s, k_s, v_s, u_s = _in_proj_sample(x_sample, mod_s, g_pre1, w_in_bf)

    attn_p = _moba_prompt(q_p, kb_p, vb_p, km_p)
    attn_s = _moba_sample(q_s, k_s, v_s, cache_k.reshape(cache_k.shape[1:]), cache_v.reshape(cache_v.shape[1:]),
                          page_table)
    attn_s = attn_s.transpose(0, 2, 1, 3).reshape(1, rows_s, ATTN_W).astype(BF16)

    cnt0 = jnp.zeros((1, LANES), F32)
    x1_p, h2_p, meta_p, gate_p, cnt_p = _post_mix(True, attn_p, u_p, None, x_prompt, mod_p, cnt0, *shared)
    x1_s, h2_s, meta_s, gate_s, cnt, conv_s = _post_mix(False, attn_s, u_s, state_conv[0],
                                                        x_sample.reshape(1, rows_s, d), mod_s, cnt_p, *shared)

    n_tok = bsz * s + rows_s
    h2 = jnp.concatenate([h2_p.reshape(bsz * s, d), h2_s.reshape(rows_s, d)], axis=0)
    meta = jnp.concatenate([meta_p.reshape(bsz * s, LANES), meta_s.reshape(rows_s, LANES)], axis=0)
    eid = meta[:, :TOP_K]
    rank = meta[:, TOP_K:2 * TOP_K]
    counts = cnt[0, :N_EXPERTS].astype(I32)
    padded = (counts + ROW_BLOCK - 1) // ROW_BLOCK * ROW_BLOCK
    pend = jnp.cumsum(padded)
    pstart = pend - padded
    pos = pstart[eid] + rank
    n_pairs = n_tok * TOP_K
    nblk = -(-n_pairs // ROW_BLOCK) + N_EXPERTS
    tok = jnp.broadcast_to(jnp.arange(n_tok, dtype=I32)[:, None], (n_tok, TOP_K))
    row_tok = jnp.zeros((nblk * ROW_BLOCK,), I32).at[pos.reshape(-1)].set(tok.reshape(-1))
    block_e = jnp.clip(jnp.searchsorted(pend, jnp.arange(nblk, dtype=I32) * ROW_BLOCK, side='right'),
                       0, N_EXPERTS - 1).astype(I32)

    xr = _moe_gather(row_tok, h2)
    yr = _moe_expert(block_e, xr, w_gate_up[0], b_gate_up[0], w_down[0], b_down[0])

    tm_c = 256
    pos_tiles = pos.reshape(n_tok // tm_c, 1, tm_c * TOP_K)
    y_p = _moe_combine(True, pos_tiles, 0, yr, gate_p, x1_p, mod_p, g_post2)
    y_s = _moe_combine(False, pos_tiles, bsz * s // tm_c, yr, gate_s, x1_s, mod_s, g_post2)

    conv_p = u_p[:, s - keep:, :]
    return (y_p, y_s.reshape(db, t, d), k_pages[None], v_pages[None], conv_p[None],
            k_s[None], v_s[None], conv_s[None])
```

```python
import functools

import jax
import jax.numpy as jnp
from jax import lax
from jax.experimental import pallas as pl
from jax.experimental.pallas import tpu as pltpu

F32 = jnp.float32
BF16 = jnp.bfloat16
I32 = jnp.int32

ATTN_HEADS = 8
HEAD_DIM = 64
ATTN_W = ATTN_HEADS * HEAD_DIM
CONV_TAPS = 31
MOBA_BLOCK = 256
MOBA_TOP = 3
PAGE_SIZE = 128
N_EXPERTS = 32
TOP_K = 4
SWIGLU_LIMIT = 7.0
SWIGLU_ALPHA = 1.702
RMS_EPS = 1e-6
LN_EPS = 1e-5
NEG_INF = -1e30
REMOVED = -3e38

LANES = 128
SUBLANES = 8
VMEM_LIMIT = 56 * 1024 * 1024

KV_STEP = 2 * MOBA_BLOCK
SAMPLE_HEADS = 4
ROW_BLOCK = 256
TOKEN_TILE = 256
META_W = 2 * TOP_K
HALO = 32


def _cparams(sem):
    return pltpu.CompilerParams(dimension_semantics=sem, vmem_limit_bytes=VMEM_LIMIT)


def _dot(a, b):
    return jnp.dot(a, b, preferred_element_type=F32)


def _dot_nt(a, b):
    return lax.dot_general(a, b, (((1,), (1,)), ((), ())), preferred_element_type=F32)


def _split(x):
    hi = x.astype(BF16)
    lo = (x - hi.astype(F32)).astype(BF16)
    return hi, lo


def _rms(x, g):
    return x * lax.rsqrt(jnp.mean(x * x, axis=-1, keepdims=True) + RMS_EPS) * g


def _top_n(score, lane_f, n):
    out = []
    work = score
    for _ in range(n):
        m = jnp.max(work, axis=1, keepdims=True)
        idx = jnp.min(jnp.where(work == m, lane_f, 1e9), axis=1, keepdims=True)
        out.append((m, idx))
        work = jnp.where(lane_f == idx, REMOVED, work)
    return out


def _mod_kernel(c_ref, w_ref, b_ref, o_ref):
    c = c_ref[...]
    s = (c * jax.nn.sigmoid(c)).astype(BF16)
    o_ref[...] = _dot(s, w_ref[...].astype(BF16)) + b_ref[...]


def _mod(c, w, b):
    n, d = c.shape
    dout = w.shape[1]
    bn = 1536
    return pl.pallas_call(
        _mod_kernel,
        grid=(dout // bn,),
        in_specs=[pl.BlockSpec((n, d), lambda j: (0, 0)),
                  pl.BlockSpec((d, bn), lambda j: (0, j)),
                  pl.BlockSpec((1, bn), lambda j: (0, j))],
        out_specs=pl.BlockSpec((n, bn), lambda j: (0, j)),
        out_shape=jax.ShapeDtypeStruct((n, dout), F32),
        compiler_params=_cparams(("arbitrary",)),
        name="mod",
    )(c, w, b.reshape(1, dout))


def _in_proj_kernel(x_ref, sh_ref, sc_ref, g_ref, w_ref, *outs, prompt):
    x = x_ref[0]
    tm = x.shape[0]
    h = _rms(x, g_ref[...]) * (1.0 + sc_ref[0]) + sh_ref[0]
    proj = _dot(h.astype(BF16), w_ref[...])
    q = proj[:, :ATTN_W] * (HEAD_DIM ** -0.5)
    k = proj[:, ATTN_W:2 * ATTN_W]
    v = proj[:, 2 * ATTN_W:3 * ATTN_W]
    c = (proj.shape[1] - 3 * ATTN_W) // 2
    a = proj[:, 3 * ATTN_W:3 * ATTN_W + c]
    g = proj[:, 3 * ATTN_W + c:]
    u = a * jax.nn.sigmoid(g)
    if prompt:
        q_ref, kb_ref, vb_ref, kf_ref, vf_ref, u_ref, km_ref = outs
        q_ref[0] = q.astype(BF16)
        kb_ref[0] = k.astype(BF16)
        vb_ref[0] = v.astype(BF16)
        u_ref[0] = u
        for p in range(tm // PAGE_SIZE):
            rows = slice(p * PAGE_SIZE, (p + 1) * PAGE_SIZE)
            for hh in range(ATTN_HEADS):
                cols = slice(hh * HEAD_DIM, (hh + 1) * HEAD_DIM)
                kf_ref[0, p, hh] = k[rows, cols]
                vf_ref[0, p, hh] = v[rows, cols]
        for blk in range(tm // MOBA_BLOCK):
            rows = slice(blk * MOBA_BLOCK, (blk + 1) * MOBA_BLOCK)
            km_ref[0, blk] = jnp.mean(k[rows], axis=0, keepdims=True)
    else:
        q_ref, kf_ref, vf_ref, u_ref = outs
        u_ref[0] = u
        t = q_ref.shape[2]
        for hh in range(ATTN_HEADS):
            cols = slice(hh * HEAD_DIM, (hh + 1) * HEAD_DIM)
            q_ref[:, hh] = q[:, cols].reshape(tm // t, t, HEAD_DIM)
            kf_ref[:, hh] = k[:, cols].reshape(tm // t, t, HEAD_DIM)
            vf_ref[:, hh] = v[:, cols].reshape(tm // t, t, HEAD_DIM)


def _mod_spec(per_row, tm, d, chunk):
    if per_row:
        return pl.BlockSpec((1, tm, d), lambda b, i: (0, i, chunk))
    return pl.BlockSpec((1, 1, d), lambda b, i: (b, 0, chunk))


def _in_proj_prompt(x, mod, g_pre1, w_in_bf):
    bsz, s, d = x.shape
    tm = 512
    c = (w_in_bf.shape[1] - 3 * ATTN_W) // 2
    n_pages = s // PAGE_SIZE
    nb = s // MOBA_BLOCK
    row = lambda w: pl.BlockSpec((1, tm, w), lambda b, i: (b, i, 0))
    paged = pl.BlockSpec((1, tm // PAGE_SIZE, ATTN_HEADS, PAGE_SIZE, HEAD_DIM), lambda b, i: (b, i, 0, 0, 0))
    return pl.pallas_call(
        functools.partial(_in_proj_kernel, prompt=True),
        grid=(bsz, s // tm),
        in_specs=[row(d), _mod_spec(False, tm, d, 0), _mod_spec(False, tm, d, 1),
                  pl.BlockSpec((1, d), lambda b, i: (0, 0)),
                  pl.BlockSpec(w_in_bf.shape, lambda b, i: (0, 0))],
        out_specs=[row(ATTN_W), row(ATTN_W), row(ATTN_W), paged, paged, row(c),
                   pl.BlockSpec((1, tm // MOBA_BLOCK, 1, ATTN_W), lambda b, i: (b, i, 0, 0))],
        out_shape=[jax.ShapeDtypeStruct((bsz, s, ATTN_W), BF16)] * 3
        + [jax.ShapeDtypeStruct((bsz, n_pages, ATTN_HEADS, PAGE_SIZE, HEAD_DIM), F32)] * 2
        + [jax.ShapeDtypeStruct((bsz, s, c), F32),
           jax.ShapeDtypeStruct((bsz, nb, 1, ATTN_W), F32)],
        compiler_params=_cparams(("arbitrary", "arbitrary")),
        name="in_proj_prompt",
    )(x, mod, mod, g_pre1, w_in_bf)


def _in_proj_sample(x, mod_rows, g_pre1, w_in_bf):
    db, t, d = x.shape
    rows = db * t
    tm = 128
    c = (w_in_bf.shape[1] - 3 * ATTN_W) // 2
    heads = pl.BlockSpec((tm // t, ATTN_HEADS, t, HEAD_DIM), lambda b, i: (i, 0, 0, 0))
    return pl.pallas_call(
        functools.partial(_in_proj_kernel, prompt=False),
        grid=(1, rows // tm),
        in_specs=[pl.BlockSpec((1, tm, d), lambda b, i: (0, i, 0)),
                  _mod_spec(True, tm, d, 0), _mod_spec(True, tm, d, 1),
                  pl.BlockSpec((1, d), lambda b, i: (0, 0)),
                  pl.BlockSpec(w_in_bf.shape, lambda b, i: (0, 0))],
        out_specs=[heads, heads, heads, pl.BlockSpec((1, tm, c), lambda b, i: (0, i, 0))],
        out_shape=[jax.ShapeDtypeStruct((db, ATTN_HEADS, t, HEAD_DIM), F32)] * 3
        + [jax.ShapeDtypeStruct((1, rows, c), F32)],
        compiler_params=_cparams(("arbitrary", "arbitrary")),
        name="in_proj_sample",
    )(x.reshape(1, rows, d), mod_rows, mod_rows, g_pre1, w_in_bf)


def _moba_prompt_kernel(q_ref, k_ref, v_ref, km_ref, o_ref):
    s_len = q_ref.shape[1]
    nb = s_len // MOBA_BLOCK
    lane = lax.broadcasted_iota(I32, (1, LANES), 1)
    lane_f = lane.astype(F32)
    row_i = lax.broadcasted_iota(I32, (MOBA_BLOCK, MOBA_BLOCK), 0)
    col_i = lax.broadcasted_iota(I32, (MOBA_BLOCK, MOBA_BLOCK), 1)
    causal = col_i <= row_i
    per = KV_STEP // MOBA_BLOCK
    key_blk = lax.broadcasted_iota(I32, (KV_STEP, 1), 0) // MOBA_BLOCK
    km = km_ref[0, :, 0, :]

    heads = []
    for hh in range(2):
        in_head = (lane >= hh * HEAD_DIM) & (lane < (hh + 1) * HEAD_DIM)
        off = HEAD_DIM * (1 - hh)
        km_h = jnp.where(in_head, km, 0.0)
        parts = []
        if off:
            parts.append(jnp.zeros((off, LANES), F32))
        parts.append(km_h)
        parts.append(jnp.zeros((LANES - off - nb, LANES), F32))
        km_hi, km_lo = _split(jnp.concatenate(parts, axis=0))
        heads.append((in_head, off, km_hi, km_lo))

    def q_tile(i, carry):
        q0 = pl.multiple_of(i * MOBA_BLOCK, MOBA_BLOCK)
        q = q_ref[0, pl.ds(q0, MOBA_BLOCK), :]
        kd = k_ref[0, pl.ds(q0, MOBA_BLOCK), :]
        vd = v_ref[0, pl.ds(q0, MOBA_BLOCK), :]
        q_augs, state = [], []
        for in_head, off, km_hi, km_lo in heads:
            qh = jnp.where(in_head, q, jnp.zeros_like(q))
            blk = lane - off
            in_bias = (blk >= 0) & (blk < nb)
            valid = in_bias & (blk < i)
            score = jnp.where(valid, _dot_nt(qh, km_hi) + _dot_nt(qh, km_lo), NEG_INF)
            sel = jnp.zeros(score.shape, F32)
            for _, idx in _top_n(score, lane_f, MOBA_TOP):
                sel = jnp.where((lane_f == idx) & valid, 1.0, sel)
            bias = jnp.where(in_bias & (sel == 0.0), NEG_INF, 0.0).astype(BF16)
            q_augs.append(jnp.where(in_head, q, bias))

            s = jnp.where(causal, _dot_nt(qh, kd), NEG_INF)
            m = jnp.max(s, axis=1, keepdims=True)
            p = jnp.exp(s - m)
            state.append((m, _dot(p.astype(BF16), jnp.where(in_head, vd, jnp.ones_like(vd)))))

        def kv_step(jj, st):
            k0 = pl.multiple_of(jj * KV_STEP, KV_STEP)
            kj = k_ref[0, pl.ds(k0, KV_STEP), :]
            vj = v_ref[0, pl.ds(k0, KV_STEP), :]
            new = []
            for (in_head, off, _, _), q_aug, (m, acc) in zip(heads, q_augs, st):
                onehot = jnp.where(lane == off + jj * per + key_blk, 1.0, 0.0).astype(BF16)
                kh = jnp.where(in_head, kj, onehot)
                vh = jnp.where(in_head, vj, jnp.ones_like(vj))
                s = _dot_nt(q_aug, kh)
                m_new = jnp.maximum(m, jnp.max(s, axis=1, keepdims=True))
                alpha = jnp.exp(m - m_new)
                p = jnp.exp(s - m_new)
                new.append((m_new, alpha * acc + _dot(p.astype(BF16), vh)))
            return tuple(new)

        state = lax.fori_loop(0, (i + per - 1) // per, kv_step, tuple(state))
        outs = [acc / pltpu.roll(acc, HEAD_DIM, axis=1) for _, acc in state]
        o_ref[0, pl.ds(q0, MOBA_BLOCK), :] = jnp.where(heads[0][0], outs[0], outs[1]).astype(o_ref.dtype)
        return carry

    lax.fori_loop(0, nb, q_tile, 0)


def _moba_prompt(q, k, v, km):
    bsz, s, _ = q.shape
    assert 2 * HEAD_DIM == LANES and s % KV_STEP == 0 and s // MOBA_BLOCK <= HEAD_DIM
    nb = s // MOBA_BLOCK
    spec = pl.BlockSpec((1, s, LANES), lambda b, hp: (b, 0, hp))
    return pl.pallas_call(
        _moba_prompt_kernel,
        grid=(bsz, ATTN_W // LANES),
        in_specs=[spec, spec, spec, pl.BlockSpec((1, nb, 1, LANES), lambda b, hp: (b, 0, 0, hp))],
        out_specs=spec,
        out_shape=jax.ShapeDtypeStruct((bsz, s, ATTN_W), BF16),
        compiler_params=_cparams(("arbitrary", "arbitrary")),
        name="moba_prompt",
    )(q, k, v, km)


def _moba_sample_kernel(pt_ref, q_ref, kn_ref, vn_ref, ck_ref, cv_ref, o_ref, kbuf, vbuf, sems):
    n_pages = kbuf.shape[1]
    hg = kbuf.shape[2]
    n_groups = pl.num_programs(1)
    seq = pl.program_id(0)
    grp = pl.program_id(1)
    step = seq * n_groups + grp
    total = pl.num_programs(0) * n_groups
    slot = step % 2

    def fetch(sq, gp, sl):
        def body(p, c):
            page = pt_ref[sq * n_pages + p]
            heads = pl.ds(gp * hg, hg)
            pltpu.make_async_copy(ck_ref.at[page, heads], kbuf.at[sl, p], sems.at[0, sl]).start()
            pltpu.make_async_copy(cv_ref.at[page, heads], vbuf.at[sl, p], sems.at[1, sl]).start()
            return c
        lax.fori_loop(0, n_pages, body, 0, unroll=4)

    @pl.when(step == 0)
    def _():
        fetch(seq, grp, slot)

    @pl.when(step + 1 < total)
    def _():
        nxt = step + 1
        fetch(nxt // n_groups, nxt % n_groups, 1 - slot)

    pltpu.make_async_copy(ck_ref.at[pl.ds(0, n_pages), pl.ds(0, hg)], kbuf.at[slot], sems.at[0, slot]).wait()
    pltpu.make_async_copy(cv_ref.at[pl.ds(0, n_pages), pl.ds(0, hg)], vbuf.at[slot], sems.at[1, slot]).wait()

    ppb = MOBA_BLOCK // PAGE_SIZE
    nfp = n_pages // ppb
    n_sel = min(MOBA_TOP, nfp)
    t = q_ref.shape[2]
    blk_id = lax.broadcasted_iota(I32, (nfp, 1, 1), 0).astype(F32)
    page_blk = (lax.broadcasted_iota(I32, (n_pages, 1, 1), 0) // ppb).astype(F32)
    causal = lax.broadcasted_iota(I32, (t, t), 1) <= lax.broadcasted_iota(I32, (t, t), 0)
    batch_nn = (((2,), (1,)), ((0,), (0,)))
    batch_nt = (((2,), (2,)), ((0,), (0,)))

    for hh in range(hg):
        qb = q_ref[0, hh].astype(BF16)
        knb = kn_ref[0, hh].astype(BF16)
        vnb = vn_ref[0, hh].astype(BF16)
        kt = kbuf[slot, :, hh].astype(BF16)
        vt = vbuf[slot, :, hh].astype(BF16)
        s = lax.dot_general(jnp.broadcast_to(qb[None], (n_pages, t, HEAD_DIM)), kt, batch_nn,
                            preferred_element_type=F32)

        score = jnp.sum(jnp.sum(s.reshape(nfp, ppb, t, PAGE_SIZE), axis=1), axis=2, keepdims=True)
        keep = None
        for _ in range(n_sel):
            best = jnp.max(score, axis=0, keepdims=True)
            idx = jnp.min(jnp.where(score == best, blk_id, 1e9), axis=0, keepdims=True)
            score = jnp.where(blk_id == idx, REMOVED, score)
            hit = page_blk == idx
            keep = hit if keep is None else keep | hit

        s = jnp.where(keep, s, NEG_INF)
        so = jnp.where(causal, _dot_nt(qb, knb), NEG_INF)
        m = jnp.maximum(jnp.max(jnp.max(s, axis=0), axis=1, keepdims=True), jnp.max(so, axis=1, keepdims=True))
        p = jnp.exp(s - m[None])
        po = jnp.exp(so - m)
        denom = jnp.sum(jnp.sum(p, axis=0), axis=1, keepdims=True) + jnp.sum(po, axis=1, keepdims=True)
        pv = lax.dot_general(p.astype(BF16), vt, batch_nt, preferred_element_type=F32)
        o_ref[0, hh] = (jnp.sum(pv, axis=0) + _dot(po.astype(BF16), vnb)) / denom


def _moba_sample(q, kn, vn, cache_kt, cache_vt, page_table):
    db, nh, t, _ = q.shape
    n_pages = page_table.shape[1]
    hg = SAMPLE_HEADS
    assert (n_pages * PAGE_SIZE) % MOBA_BLOCK == 0 and n_pages * PAGE_SIZE >= MOBA_BLOCK and nh % hg == 0
    new = pl.BlockSpec((1, hg, t, HEAD_DIM), lambda s, g, pt: (s, g, 0, 0))
    return pl.pallas_call(
        _moba_sample_kernel,
        grid_spec=pltpu.PrefetchScalarGridSpec(
            num_scalar_prefetch=1,
            grid=(db, nh // hg),
            in_specs=[new, new, new, pl.BlockSpec(memory_space=pl.ANY), pl.BlockSpec(memory_space=pl.ANY)],
            out_specs=new,
            scratch_shapes=[pltpu.VMEM((2, n_pages, hg, HEAD_DIM, PAGE_SIZE), F32),
                            pltpu.VMEM((2, n_pages, hg, HEAD_DIM, PAGE_SIZE), F32),
                            pltpu.SemaphoreType.DMA((2, 2))]),
        out_shape=jax.ShapeDtypeStruct((db, nh, t, HEAD_DIM), F32),
        compiler_params=_cparams(("arbitrary", "arbitrary")),
        name="moba_sample",
    )(page_table.reshape(-1), q, kn, vn, cache_kt, cache_vt)


def _conv_tail(y, lng_ref, lnb_ref):
    mu = jnp.mean(y, axis=-1, keepdims=True)
    yc = y - mu
    var = jnp.mean(yc * yc, axis=-1, keepdims=True)
    z = yc * lax.rsqrt(var + LN_EPS) * lng_ref[...] + lnb_ref[...]
    return z * jax.nn.sigmoid(z)


def _post_mix_tail(attn, conv, x, gt1, sh2, sc2, wo_ref, bo_ref, gp1_ref, gp2_ref, rw_ref, rb_ref,
                   cnt_ref, x1_ref, h2_ref, meta_ref, gate_ref):
    tm = x.shape[0]
    half = attn.shape[1]
    mixed = _dot(attn, wo_ref[:half, :]) + _dot(conv.astype(BF16), wo_ref[half:, :]) + bo_ref[...]
    x1 = x + gt1 * _rms(mixed, gp1_ref[...])
    h2 = _rms(x1, gp2_ref[...]) * (1.0 + sc2) + sh2
    x1_ref[...] = x1.reshape(x1_ref.shape)
    h2_ref[...] = h2.reshape(h2_ref.shape)

    lane = lax.broadcasted_iota(I32, (1, LANES), 1)
    lane_f = lane.astype(F32)
    h_hi, h_lo = _split(h2)
    w_hi, w_lo = _split(rw_ref[...])
    logits = _dot(h_hi, w_hi) + _dot(h_hi, w_lo) + _dot(h_lo, w_hi) + rb_ref[...]
    logits = jnp.where(lane < N_EXPERTS, logits, REMOVED)
    picks = _top_n(logits, lane_f, TOP_K)
    exps = [jnp.exp(val - picks[0][0]) for val, _ in picks]
    total = exps[0]
    for e in exps[1:]:
        total = total + e
    sel = jnp.zeros((tm, LANES), F32)
    for _, idx in picks:
        sel = jnp.where(lane_f == idx, 1.0, sel)

    before = lax.broadcasted_iota(I32, (tm, tm), 1) < lax.broadcasted_iota(I32, (tm, tm), 0)
    rank_all = cnt_ref[...] + _dot(jnp.where(before, 1.0, 0.0).astype(BF16), sel.astype(BF16))
    cnt_ref[...] = cnt_ref[...] + jnp.sum(sel, axis=0, keepdims=True)

    meta = jnp.zeros((tm, LANES), F32)
    gates = jnp.zeros((tm, LANES), F32)
    for kk, (_, idx) in enumerate(picks):
        rank = jnp.sum(jnp.where(lane_f == idx, rank_all, 0.0), axis=1, keepdims=True)
        meta = jnp.where(lane == kk, idx, meta)
        meta = jnp.where(lane == TOP_K + kk, rank, meta)
        gates = jnp.where(lane == kk, exps[kk] / total, gates)
    meta_ref[...] = meta.astype(I32).reshape(meta_ref.shape)
    gate_ref[...] = gates.reshape(gate_ref.shape)


def _post_mix_prompt_kernel(attn_ref, u_ref, halo_ref, x_ref, gt1_ref, sh2_ref, sc2_ref, cw_ref, cb_ref,
                            lng_ref, lnb_ref, wo_ref, bo_ref, gp1_ref, gp2_ref, rw_ref, rb_ref, cnt0_ref,
                            x1_ref, h2_ref, meta_ref, gate_ref, cnt_ref, ext):
    b = pl.program_id(0)
    i = pl.program_id(1)
    tm = u_ref.shape[1]

    @pl.when((b == 0) & (i == 0))
    def _():
        cnt_ref[...] = cnt0_ref[...]

    ext[0:HALO, :] = jnp.where(i > 0, halo_ref[0], 0.0)
    ext[HALO:HALO + tm, :] = u_ref[0]
    first = HALO - (CONV_TAPS - 1)
    y = cb_ref[...] + cw_ref[0:1, :] * ext[first:first + tm, :]
    for j in range(1, CONV_TAPS):
        y = y + cw_ref[j:j + 1, :] * ext[first + j:first + j + tm, :]
    conv = _conv_tail(y, lng_ref, lnb_ref)
    _post_mix_tail(attn_ref[0], conv, x_ref[0], gt1_ref[0], sh2_ref[0], sc2_ref[0], wo_ref, bo_ref,
                   gp1_ref, gp2_ref, rw_ref, rb_ref, cnt_ref, x1_ref, h2_ref, meta_ref, gate_ref)


def _post_mix_sample_kernel(attn_ref, u_ref, st_ref, x_ref, gt1_ref, sh2_ref, sc2_ref, cw_ref, cb_ref,
                            lng_ref, lnb_ref, wo_ref, bo_ref, gp1_ref, gp2_ref, rw_ref, rb_ref, cnt0_ref,
                            x1_ref, h2_ref, meta_ref, gate_ref, cnt_ref, st_out_ref, ext):
    i = pl.program_id(1)
    nseq, keep, c = st_ref.shape
    t = u_ref.shape[1] // nseq

    @pl.when(i == 0)
    def _():
        cnt_ref[...] = cnt0_ref[...]

    ext[:, 0:keep, :] = st_ref[...]
    ext[:, keep:keep + t, :] = u_ref[0].reshape(nseq, t, c)
    y = cb_ref[...] + cw_ref[0:1, :] * ext[:, 0:t, :]
    for j in range(1, CONV_TAPS):
        y = y + cw_ref[j:j + 1, :] * ext[:, j:j + t, :]
    st_out_ref[...] = ext[:, t:t + keep, :]
    conv = _conv_tail(y.reshape(nseq * t, c), lng_ref, lnb_ref)
    _post_mix_tail(attn_ref[0], conv, x_ref[0], gt1_ref[0], sh2_ref[0], sc2_ref[0], wo_ref, bo_ref,
                   gp1_ref, gp2_ref, rw_ref, rb_ref, cnt_ref, x1_ref, h2_ref, meta_ref, gate_ref)


def _post_mix(prompt, attn, u, prefix, x, mod, cnt0, cw, cb, lng, lnb, wo_bf, bo, gp1, gp2, rw, rb):
    bsz, s, d = x.shape
    c = u.shape[2]
    tm = 256 if prompt else 128
    row = lambda w: pl.BlockSpec((1, tm, w), lambda b, i: (b, i, 0))
    full = lambda a: pl.BlockSpec(a.shape, lambda b, i: (0,) * a.ndim)
    weights = (cw, cb, lng, lnb, wo_bf, bo, gp1, gp2, rw, rb, cnt0)
    mods = [_mod_spec(not prompt, tm, d, ch) for ch in (2, 3, 4)]
    out_specs = [row(d), row(d), row(LANES), row(LANES), pl.BlockSpec((1, LANES), lambda b, i: (0, 0))]
    out_shape = [jax.ShapeDtypeStruct((bsz, s, d), F32), jax.ShapeDtypeStruct((bsz, s, d), F32),
                 jax.ShapeDtypeStruct((bsz, s, LANES), I32), jax.ShapeDtypeStruct((bsz, s, LANES), F32),
                 jax.ShapeDtypeStruct((1, LANES), F32)]
    if prompt:
        kern = _post_mix_prompt_kernel
        per = tm // HALO
        side = pl.BlockSpec((1, HALO, c), lambda b, i: (b, jnp.maximum(i * per - 1, 0), 0))
        side_arg = u
        scratch = [pltpu.VMEM((HALO + tm, c), F32)]
    else:
        kern = _post_mix_sample_kernel
        nseq, keep = prefix.shape[0], prefix.shape[1]
        t = s // nseq
        per = tm // t
        side = pl.BlockSpec((per, keep, c), lambda b, i: (i, 0, 0))
        side_arg = prefix
        out_specs.append(pl.BlockSpec((per, keep, c), lambda b, i: (i, 0, 0)))
        out_shape.append(jax.ShapeDtypeStruct(prefix.shape, F32))
        scratch = [pltpu.VMEM((per, -(-(keep + t) // SUBLANES) * SUBLANES, c), F32)]
    return pl.pallas_call(
        kern,
        grid=(bsz, s // tm),
        in_specs=[row(ATTN_W), row(c), side, row(d)] + mods + [full(a) for a in weights],
        out_specs=out_specs,
        out_shape=out_shape,
        scratch_shapes=scratch,
        compiler_params=_cparams(("arbitrary", "arbitrary")),
        name="post_mix_prompt" if prompt else "post_mix_sample",
    )(attn, u, side_arg, x, mod, mod, mod, *weights)


def _meta_spec(index):
    return pl.BlockSpec((1, 1, TOKEN_TILE * META_W), index, memory_space=pltpu.SMEM)


def _moe_scatter_kernel(cnt_ref, pst_ref, meta_ref, hp_ref, hs_ref, xr_ref, sem):
    i = pl.program_id(0)
    n_rows = xr_ref.shape[0]
    tiles_p = hp_ref.shape[0] // TOKEN_TILE
    n_tok = hp_ref.shape[0] + hs_ref.shape[0]
    n_pad = n_rows - n_tok * TOP_K

    def rows_copy(src_ref, n):
        return pltpu.make_async_copy(src_ref.at[pl.ds(0, n)], xr_ref.at[pl.ds(0, n)], sem)

    @pl.when(i == 0)
    def _():
        def fill(lo, hi):
            def body(r, c):
                pltpu.make_async_copy(hp_ref.at[pl.ds(0, 1)], xr_ref.at[pl.ds(r, 1)], sem).start()
                return c
            lax.fori_loop(lo, hi, body, 0)

        end = 0
        for e in range(N_EXPERTS):
            used = pst_ref[e] + cnt_ref[e]
            end = pst_ref[e] + (cnt_ref[e] + ROW_BLOCK - 1) // ROW_BLOCK * ROW_BLOCK
            fill(used, end)
        fill(end, n_rows)
        for _ in range(n_pad // ROW_BLOCK):
            rows_copy(hp_ref, ROW_BLOCK).wait()

    def scatter_tile(src_ref, tile):
        def body(r, c):
            for kk in range(TOP_K):
                dst = pst_ref[meta_ref[0, 0, r * META_W + kk]] + meta_ref[0, 0, r * META_W + TOP_K + kk]
                pltpu.make_async_copy(src_ref.at[pl.ds(tile * TOKEN_TILE + r, 1)], xr_ref.at[pl.ds(dst, 1)],
                                      sem).start()
            return c
        lax.fori_loop(0, TOKEN_TILE, body, 0, unroll=2)

    @pl.when(i < tiles_p)
    def _():
        scatter_tile(hp_ref, i)

    @pl.when(i >= tiles_p)
    def _():
        scatter_tile(hs_ref, i - tiles_p)

    @pl.when(i > 0)
    def _():
        rows_copy(hp_ref, TOKEN_TILE * TOP_K).wait()

    @pl.when(i == pl.num_programs(0) - 1)
    def _():
        rows_copy(hp_ref, TOKEN_TILE * TOP_K).wait()


def _moe_scatter(counts, pstart, meta_tiles, h_p, h_s, n_rows):
    d = h_p.shape[1]
    assert h_p.shape[0] % TOKEN_TILE == 0 and h_s.shape[0] % TOKEN_TILE == 0
    assert (n_rows - (h_p.shape[0] + h_s.shape[0]) * TOP_K) % ROW_BLOCK == 0 and h_p.shape[0] >= TOKEN_TILE * TOP_K
    return pl.pallas_call(
        _moe_scatter_kernel,
        grid_spec=pltpu.PrefetchScalarGridSpec(
            num_scalar_prefetch=2,
            grid=(meta_tiles.shape[0],),
            in_specs=[_meta_spec(lambda i, cnt, pst: (i, 0, 0)),
                      pl.BlockSpec(memory_space=pl.ANY), pl.BlockSpec(memory_space=pl.ANY)],
            out_specs=pl.BlockSpec(memory_space=pl.ANY),
            scratch_shapes=[pltpu.SemaphoreType.DMA(())]),
        out_shape=jax.ShapeDtypeStruct((n_rows, d), F32),
        compiler_params=_cparams(("arbitrary",)),
        name="moe_scatter",
    )(counts, pstart, meta_tiles, h_p, h_s)


def _moe_expert_kernel(be_ref, na_ref, x_ref, wgu_ref, bgu_ref, wd_ref, bd_ref, y_ref, wgu_bf, wd_bf):
    b = pl.program_id(0)
    active = b < na_ref[0]
    prev = be_ref[jnp.maximum(b - 1, 0)]

    @pl.when(active & ((b == 0) | (be_ref[b] != prev)))
    def _():
        wgu_bf[...] = wgu_ref[0].astype(BF16)
        wd_bf[...] = wd_ref[0].astype(BF16)

    @pl.when(active)
    def _():
        de = wd_bf.shape[0]
        gu = _dot(x_ref[...].astype(BF16), wgu_bf[...]) + bgu_ref[0]
        glu = jnp.minimum(gu[:, :de], SWIGLU_LIMIT)
        lin = jnp.clip(gu[:, de:], -SWIGLU_LIMIT, SWIGLU_LIMIT)
        act = glu * jax.nn.sigmoid(SWIGLU_ALPHA * glu) * (lin + 1.0)
        y_ref[...] = _dot(act.astype(BF16), wd_bf[...]) + bd_ref[0]

    @pl.when(jnp.logical_not(active))
    def _():
        y_ref[...] = jnp.zeros(y_ref.shape, y_ref.dtype)


def _moe_expert(block_e, n_active, xr, wgu, bgu, wd, bd):
    n_rows, d = xr.shape
    ne, _, two_de = wgu.shape
    de = wd.shape[1]
    nblk = n_rows // ROW_BLOCK
    last = lambda b, na: jnp.minimum(b, na[0] - 1)
    return pl.pallas_call(
        _moe_expert_kernel,
        grid_spec=pltpu.PrefetchScalarGridSpec(
            num_scalar_prefetch=2,
            grid=(nblk,),
            in_specs=[pl.BlockSpec((ROW_BLOCK, d), lambda b, be, na: (last(b, na), 0)),
                      pl.BlockSpec((1, d, two_de), lambda b, be, na: (be[last(b, na)], 0, 0)),
                      pl.BlockSpec((1, 1, two_de), lambda b, be, na: (be[last(b, na)], 0, 0)),
                      pl.BlockSpec((1, de, d), lambda b, be, na: (be[last(b, na)], 0, 0)),
                      pl.BlockSpec((1, 1, d), lambda b, be, na: (be[last(b, na)], 0, 0))],
            out_specs=pl.BlockSpec((ROW_BLOCK, d), lambda b, be, na: (b, 0)),
            scratch_shapes=[pltpu.VMEM((d, two_de), BF16), pltpu.VMEM((de, d), BF16)]),
        out_shape=jax.ShapeDtypeStruct((n_rows, d), F32),
        compiler_params=_cparams(("arbitrary",)),
        name="moe_expert",
    )(block_e, n_active, xr, wgu, bgu.reshape(ne, 1, two_de), wd, bd.reshape(ne, 1, d))


def _moe_combine_kernel(pst_ref, meta_ref, nmeta_ref, yr_ref, gate_ref, x1_ref, gt2_ref, g_ref, o_ref, buf, sems):
    step = pl.program_id(0) * pl.num_programs(1) + pl.program_id(1)
    total = pl.num_programs(0) * pl.num_programs(1)
    slot = step % 2
    tm = buf.shape[2]

    def gather(m_ref, sl):
        def body(r, c):
            for kk in range(TOP_K):
                src = pst_ref[m_ref[0, 0, r * META_W + kk]] + m_ref[0, 0, r * META_W + TOP_K + kk]
                pltpu.make_async_copy(yr_ref.at[pl.ds(src, 1)], buf.at[sl, kk, pl.ds(r, 1)], sems.at[sl]).start()
            return c
        lax.fori_loop(0, tm, body, 0, unroll=2)

    @pl.when(step == 0)
    def _():
        gather(meta_ref, slot)

    @pl.when(step + 1 < total)
    def _():
        gather(nmeta_ref, 1 - slot)

    for kk in range(TOP_K):
        pltpu.make_async_copy(yr_ref.at[pl.ds(0, tm)], buf.at[slot, kk], sems.at[slot]).wait()
    gates = gate_ref[0]
    f = gates[:, 0:1] * buf[slot, 0]
    for kk in range(1, TOP_K):
        f = f + gates[:, kk:kk + 1] * buf[slot, kk]
    o_ref[0] = x1_ref[0] + gt2_ref[0] * _rms(f, g_ref[...])


def _moe_combine(prompt, pstart, meta_tiles, tile0, yr, gates, x1, mod, g_post2):
    bsz, s, d = x1.shape
    tm = TOKEN_TILE
    per_b = s // tm
    last = tile0 + bsz * per_b - 1
    row = lambda w: pl.BlockSpec((1, tm, w), lambda b, i, pst: (b, i, 0))
    if prompt:
        gt2 = pl.BlockSpec((1, 1, d), lambda b, i, pst: (b, 0, 5))
    else:
        gt2 = pl.BlockSpec((1, tm, d), lambda b, i, pst: (0, i, 5))
    return pl.pallas_call(
        _moe_combine_kernel,
        grid_spec=pltpu.PrefetchScalarGridSpec(
            num_scalar_prefetch=1,
            grid=(bsz, per_b),
            in_specs=[_meta_spec(lambda b, i, pst: (tile0 + b * per_b + i, 0, 0)),
                      _meta_spec(lambda b, i, pst: (jnp.minimum(tile0 + b * per_b + i + 1, last), 0, 0)),
                      pl.BlockSpec(memory_space=pl.ANY),
                      row(LANES), row(d), gt2,
                      pl.BlockSpec((1, d), lambda b, i, pst: (0, 0))],
            out_specs=row(d),
            scratch_shapes=[pltpu.VMEM((2, TOP_K, tm, d), F32), pltpu.SemaphoreType.DMA((2,))]),
        out_shape=jax.ShapeDtypeStruct((bsz, s, d), F32),
        compiler_params=_cparams(("arbitrary", "arbitrary")),
        name="moe_combine_prompt" if prompt else "moe_combine_sample",
    )(pstart, meta_tiles, meta_tiles, yr, gates, x1, mod, g_post2)


def kernel(x_prompt, x_sample, cache_k, cache_v, state_conv, page_table, c_prompt, c_sample, w_mod, b_mod,
           g_pre1, g_post1, g_pre2, g_post2, w_in, conv_w, conv_b, conv_ln_g, conv_ln_b, w_out, b_out,
           router_w, router_b, w_gate_up, b_gate_up, w_down, b_down):
    assert w_mod.shape[0] == 1, "single-layer trunk"
    bsz, s, d = x_prompt.shape
    db, t, _ = x_sample.shape
    c = conv_w.shape[2]
    keep = CONV_TAPS - 1
    rows_s = db * t

    mod = _mod(jnp.concatenate([c_prompt, c_sample], axis=0), w_mod[0], b_mod[0])
    mod_p = mod[:bsz].reshape(bsz, 1, 6 * d)
    mod_s = jnp.repeat(mod[bsz:], t, axis=0).reshape(1, rows_s, 6 * d)

    w_in_bf = w_in[0].astype(BF16)
    wo_bf = w_out[0].astype(BF16)
    cw = jnp.concatenate([conv_w[0], jnp.zeros((HALO - CONV_TAPS, c), F32)], axis=0)
    rw = jnp.concatenate([router_w[0], jnp.zeros((d, LANES - N_EXPERTS), F32)], axis=1)
    rb = jnp.concatenate([router_b[0], jnp.zeros((LANES - N_EXPERTS,), F32)]).reshape(1, LANES)
    shared = (cw, conv_b, conv_ln_g, conv_ln_b, wo_bf, b_out, g_post1, g_pre2, rw, rb)

    q_p, kb_p, vb_p, k_pages, v_pages, u_p, km_p = _in_proj_prompt(x_prompt, mod_p, g_pre1, w_in_bf)
    q_s, k_s, v_s, u_s = _in_proj_sample(x_sample, mod_s, g_pre1, w_in_bf)

    attn_p = _moba_prompt(q_p, kb_p, vb_p, km_p)
    attn_s = _moba_sample(q_s, k_s, v_s, jnp.swapaxes(cache_k[0], 2, 3), jnp.swapaxes(cache_v[0], 2, 3), page_table)
    attn_s = attn_s.transpose(0, 2, 1, 3).reshape(1, rows_s, ATTN_W).astype(BF16)

    cnt0 = jnp.zeros((1, LANES), F32)
    x1_p, h2_p, meta_p, gate_p, cnt_p = _post_mix(True, attn_p, u_p, None, x_prompt, mod_p, cnt0, *shared)
    x1_s, h2_s, meta_s, gate_s, cnt, conv_s = _post_mix(False, attn_s, u_s, state_conv[0],
                                                        x_sample.reshape(1, rows_s, d), mod_s, cnt_p, *shared)

    n_tok = bsz * s + rows_s
    counts = cnt[0, :N_EXPERTS].astype(I32)
    padded = (counts + ROW_BLOCK - 1) // ROW_BLOCK * ROW_BLOCK
    pend = jnp.cumsum(padded)
    pstart = pend - padded
    nblk = -(-(n_tok * TOP_K) // ROW_BLOCK) + N_EXPERTS
    starts = jnp.arange(nblk, dtype=I32) * ROW_BLOCK
    block_e = jnp.minimum(jnp.sum((pend[None, :] <= starts[:, None]).astype(I32), axis=1), N_EXPERTS - 1)
    n_active = (pend[-1:] // ROW_BLOCK).astype(I32)
    meta = jnp.concatenate([meta_p.reshape(bsz * s, LANES)[:, :META_W], meta_s.reshape(rows_s, LANES)[:, :META_W]],
                           axis=0)
    meta_tiles = meta.reshape(n_tok // TOKEN_TILE, 1, TOKEN_TILE * META_W)

    xr = _moe_scatter(counts, pstart, meta_tiles, h2_p.reshape(bsz * s, d), h2_s.reshape(rows_s, d),
                      nblk * ROW_BLOCK)
    yr = _moe_expert(block_e, n_active, xr, w_gate_up[0], b_gate_up[0], w_down[0], b_down[0])
    y_p = _moe_combine(True, pstart, meta_tiles, 0, yr, gate_p, x1_p, mod_p, g_post2)
    y_s = _moe_combine(False, pstart, meta_tiles, bsz * s // TOKEN_TILE, yr, gate_s, x1_s, mod_s, g_post2)

    conv_p = u_p[:, s - keep:, :]
    return (y_p, y_s.reshape(db, t, d), k_pages[None], v_pages[None], conv_p[None],
            k_s[None], v_s[None], conv_s[None])
```

```python
import functools

import jax
import jax.numpy as jnp
from jax import lax
from jax.experimental import pallas as pl
from jax.experimental.pallas import tpu as pltpu

F32 = jnp.float32
BF16 = jnp.bfloat16
I32 = jnp.int32

ATTN_HEADS = 8
HEAD_DIM = 64
ATTN_W = ATTN_HEADS * HEAD_DIM
CONV_TAPS = 31
MOBA_BLOCK = 256
MOBA_TOP = 3
PAGE_SIZE = 128
N_EXPERTS = 32
TOP_K = 4
SWIGLU_LIMIT = 7.0
SWIGLU_ALPHA = 1.702
RMS_EPS = 1e-6
LN_EPS = 1e-5
NEG_INF = -1e30
REMOVED = -3e38

LANES = 128
SUBLANES = 8
VMEM_LIMIT = 56 * 1024 * 1024

KV_STEP = 4 * MOBA_BLOCK
SAMPLE_HEADS = 4
ROW_BLOCK = 256
TOKEN_TILE = 256
META_W = 2 * TOP_K
HALO = 32


def _cparams(sem):
    return pltpu.CompilerParams(dimension_semantics=sem, vmem_limit_bytes=VMEM_LIMIT)


def _dot(a, b):
    return jnp.dot(a, b, preferred_element_type=F32)


def _dot_nt(a, b):
    return lax.dot_general(a, b, (((1,), (1,)), ((), ())), preferred_element_type=F32)


def _split(x):
    hi = x.astype(BF16)
    lo = (x - hi.astype(F32)).astype(BF16)
    return hi, lo


def _rms(x, g):
    return x * lax.rsqrt(jnp.mean(x * x, axis=-1, keepdims=True) + RMS_EPS) * g


def _store_rows_as_tiles(ref, x):
    rows = x.shape[0]
    for c in range(SUBLANES):
        ref[pl.ds(c, rows, stride=SUBLANES), :] = x[:, c * LANES:(c + 1) * LANES]


def _load_rows_from_tiles(ref, rows, lead=()):
    return jnp.concatenate([ref[(*lead, pl.ds(c, rows, stride=SUBLANES), slice(None))] for c in range(SUBLANES)],
                           axis=1)


def _top_n(score, lane_f, n):
    out = []
    work = score
    for _ in range(n):
        m = jnp.max(work, axis=1, keepdims=True)
        idx = jnp.min(jnp.where(work == m, lane_f, 1e9), axis=1, keepdims=True)
        out.append((m, idx))
        work = jnp.where(lane_f == idx, REMOVED, work)
    return out


def _mod_kernel(c_ref, w_ref, b_ref, o_ref):
    c = c_ref[...]
    s = (c * jax.nn.sigmoid(c)).astype(BF16)
    o_ref[...] = _dot(s, w_ref[...].astype(BF16)) + b_ref[...]


def _mod(c, w, b):
    n, d = c.shape
    dout = w.shape[1]
    bn = 1536
    return pl.pallas_call(
        _mod_kernel,
        grid=(dout // bn,),
        in_specs=[pl.BlockSpec((n, d), lambda j: (0, 0)),
                  pl.BlockSpec((d, bn), lambda j: (0, j)),
                  pl.BlockSpec((1, bn), lambda j: (0, j))],
        out_specs=pl.BlockSpec((n, bn), lambda j: (0, j)),
        out_shape=jax.ShapeDtypeStruct((n, dout), F32),
        compiler_params=_cparams(("arbitrary",)),
        name="mod",
    )(c, w, b.reshape(1, dout))


def _in_proj_kernel(x_ref, sh_ref, sc_ref, g_ref, w_ref, *outs, prompt):
    x = x_ref[0]
    tm = x.shape[0]
    h = _rms(x, g_ref[...]) * (1.0 + sc_ref[0]) + sh_ref[0]
    proj = _dot(h.astype(BF16), w_ref[...])
    q = proj[:, :ATTN_W] * (HEAD_DIM ** -0.5)
    k = proj[:, ATTN_W:2 * ATTN_W]
    v = proj[:, 2 * ATTN_W:3 * ATTN_W]
    c = (proj.shape[1] - 3 * ATTN_W) // 2
    a = proj[:, 3 * ATTN_W:3 * ATTN_W + c]
    g = proj[:, 3 * ATTN_W + c:]
    u = a * jax.nn.sigmoid(g)
    if prompt:
        q_ref, kb_ref, vb_ref, kf_ref, vf_ref, u_ref, km_ref = outs
        q_ref[0] = q.astype(BF16)
        kb_ref[0] = k.astype(BF16)
        vb_ref[0] = v.astype(BF16)
        u_ref[0] = u
        for p in range(tm // PAGE_SIZE):
            rows = slice(p * PAGE_SIZE, (p + 1) * PAGE_SIZE)
            for hh in range(ATTN_HEADS):
                cols = slice(hh * HEAD_DIM, (hh + 1) * HEAD_DIM)
                kf_ref[0, p, hh] = k[rows, cols]
                vf_ref[0, p, hh] = v[rows, cols]
        for blk in range(tm // MOBA_BLOCK):
            rows = slice(blk * MOBA_BLOCK, (blk + 1) * MOBA_BLOCK)
            km_ref[0, blk] = jnp.mean(k[rows], axis=0, keepdims=True)
    else:
        q_ref, kf_ref, vf_ref, u_ref = outs
        u_ref[0] = u
        t = q_ref.shape[2]
        for hh in range(ATTN_HEADS):
            cols = slice(hh * HEAD_DIM, (hh + 1) * HEAD_DIM)
            q_ref[:, hh] = q[:, cols].reshape(tm // t, t, HEAD_DIM)
            kf_ref[:, hh] = k[:, cols].reshape(tm // t, t, HEAD_DIM)
            vf_ref[:, hh] = v[:, cols].reshape(tm // t, t, HEAD_DIM)


def _mod_spec(per_row, tm, d, chunk):
    if per_row:
        return pl.BlockSpec((1, tm, d), lambda b, i: (0, i, chunk))
    return pl.BlockSpec((1, 1, d), lambda b, i: (b, 0, chunk))


def _in_proj_prompt(x, mod, g_pre1, w_in_bf):
    bsz, s, d = x.shape
    tm = 512
    c = (w_in_bf.shape[1] - 3 * ATTN_W) // 2
    n_pages = s // PAGE_SIZE
    nb = s // MOBA_BLOCK
    row = lambda w: pl.BlockSpec((1, tm, w), lambda b, i: (b, i, 0))
    paged = pl.BlockSpec((1, tm // PAGE_SIZE, ATTN_HEADS, PAGE_SIZE, HEAD_DIM), lambda b, i: (b, i, 0, 0, 0))
    return pl.pallas_call(
        functools.partial(_in_proj_kernel, prompt=True),
        grid=(bsz, s // tm),
        in_specs=[row(d), _mod_spec(False, tm, d, 0), _mod_spec(False, tm, d, 1),
                  pl.BlockSpec((1, d), lambda b, i: (0, 0)),
                  pl.BlockSpec(w_in_bf.shape, lambda b, i: (0, 0))],
        out_specs=[row(ATTN_W), row(ATTN_W), row(ATTN_W), paged, paged, row(c),
                   pl.BlockSpec((1, tm // MOBA_BLOCK, 1, ATTN_W), lambda b, i: (b, i, 0, 0))],
        out_shape=[jax.ShapeDtypeStruct((bsz, s, ATTN_W), BF16)] * 3
        + [jax.ShapeDtypeStruct((bsz, n_pages, ATTN_HEADS, PAGE_SIZE, HEAD_DIM), F32)] * 2
        + [jax.ShapeDtypeStruct((bsz, s, c), F32),
           jax.ShapeDtypeStruct((bsz, nb, 1, ATTN_W), F32)],
        compiler_params=_cparams(("arbitrary", "arbitrary")),
        name="in_proj_prompt",
    )(x, mod, mod, g_pre1, w_in_bf)


def _in_proj_sample(x, mod_rows, g_pre1, w_in_bf):
    db, t, d = x.shape
    rows = db * t
    tm = 128
    c = (w_in_bf.shape[1] - 3 * ATTN_W) // 2
    heads = pl.BlockSpec((tm // t, ATTN_HEADS, t, HEAD_DIM), lambda b, i: (i, 0, 0, 0))
    return pl.pallas_call(
        functools.partial(_in_proj_kernel, prompt=False),
        grid=(1, rows // tm),
        in_specs=[pl.BlockSpec((1, tm, d), lambda b, i: (0, i, 0)),
                  _mod_spec(True, tm, d, 0), _mod_spec(True, tm, d, 1),
                  pl.BlockSpec((1, d), lambda b, i: (0, 0)),
                  pl.BlockSpec(w_in_bf.shape, lambda b, i: (0, 0))],
        out_specs=[heads, heads, heads, pl.BlockSpec((1, tm, c), lambda b, i: (0, i, 0))],
        out_shape=[jax.ShapeDtypeStruct((db, ATTN_HEADS, t, HEAD_DIM), F32)] * 3
        + [jax.ShapeDtypeStruct((1, rows, c), F32)],
        compiler_params=_cparams(("arbitrary", "arbitrary")),
        name="in_proj_sample",
    )(x.reshape(1, rows, d), mod_rows, mod_rows, g_pre1, w_in_bf)


def _moba_prompt_kernel(q_ref, k_ref, v_ref, km_ref, o_ref):
    s_len = q_ref.shape[1]
    nb = s_len // MOBA_BLOCK
    lane = lax.broadcasted_iota(I32, (1, LANES), 1)
    lane_f = lane.astype(F32)
    row_i = lax.broadcasted_iota(I32, (MOBA_BLOCK, MOBA_BLOCK), 0)
    col_i = lax.broadcasted_iota(I32, (MOBA_BLOCK, MOBA_BLOCK), 1)
    causal = col_i <= row_i
    per = KV_STEP // MOBA_BLOCK
    key_blk = lax.broadcasted_iota(I32, (KV_STEP, 1), 0) // MOBA_BLOCK
    km = km_ref[0, :, 0, :]

    heads = []
    for hh in range(2):
        in_head = (lane >= hh * HEAD_DIM) & (lane < (hh + 1) * HEAD_DIM)
        off = HEAD_DIM * (1 - hh)
        km_h = jnp.where(in_head, km, 0.0)
        parts = []
        if off:
            parts.append(jnp.zeros((off, LANES), F32))
        parts.append(km_h)
        parts.append(jnp.zeros((LANES - off - nb, LANES), F32))
        km_hi, km_lo = _split(jnp.concatenate(parts, axis=0))
        heads.append((in_head, off, km_hi, km_lo))

    def q_tile(i, carry):
        q0 = pl.multiple_of(i * MOBA_BLOCK, MOBA_BLOCK)
        q = q_ref[0, pl.ds(q0, MOBA_BLOCK), :]
        kd = k_ref[0, pl.ds(q0, MOBA_BLOCK), :]
        vd = v_ref[0, pl.ds(q0, MOBA_BLOCK), :]
        q_augs, state = [], []
        for in_head, off, km_hi, km_lo in heads:
            qh = jnp.where(in_head, q, jnp.zeros_like(q))
            blk = lane - off
            in_bias = (blk >= 0) & (blk < nb)
            valid = in_bias & (blk < i)
            score = jnp.where(valid, _dot_nt(qh, km_hi) + _dot_nt(qh, km_lo), NEG_INF)
            sel = jnp.zeros(score.shape, F32)
            for _, idx in _top_n(score, lane_f, MOBA_TOP):
                sel = jnp.where((lane_f == idx) & valid, 1.0, sel)
            bias = jnp.where(in_bias & (sel == 0.0), NEG_INF, 0.0).astype(BF16)
            q_augs.append(jnp.where(in_head, q, bias))

            s = jnp.where(causal, _dot_nt(qh, kd), NEG_INF)
            m = jnp.max(s, axis=1, keepdims=True)
            p = jnp.exp(s - m)
            state.append((m, _dot(p.astype(BF16), jnp.where(in_head, vd, jnp.ones_like(vd)))))

        def kv_step(jj, st):
            k0 = pl.multiple_of(jj * KV_STEP, KV_STEP)
            kj = k_ref[0, pl.ds(k0, KV_STEP), :]
            vj = v_ref[0, pl.ds(k0, KV_STEP), :]
            new = []
            for (in_head, off, _, _), q_aug, (m, acc) in zip(heads, q_augs, st):
                onehot = jnp.where(lane == off + jj * per + key_blk, 1.0, 0.0).astype(BF16)
                kh = jnp.where(in_head, kj, onehot)
                vh = jnp.where(in_head, vj, jnp.ones_like(vj))
                s = _dot_nt(q_aug, kh)
                m_new = jnp.maximum(m, jnp.max(s, axis=1, keepdims=True))
                alpha = jnp.exp(m - m_new)
                p = jnp.exp(s - m_new)
                new.append((m_new, alpha * acc + _dot(p.astype(BF16), vh)))
            return tuple(new)

        state = lax.fori_loop(0, (i + per - 1) // per, kv_step, tuple(state))
        outs = [acc / pltpu.roll(acc, HEAD_DIM, axis=1) for _, acc in state]
        o_ref[0, pl.ds(q0, MOBA_BLOCK), :] = jnp.where(heads[0][0], outs[0], outs[1]).astype(o_ref.dtype)
        return carry

    lax.fori_loop(0, nb, q_tile, 0)


def _moba_prompt(q, k, v, km):
    bsz, s, _ = q.shape
    assert 2 * HEAD_DIM == LANES and s % KV_STEP == 0 and s // MOBA_BLOCK <= HEAD_DIM
    nb = s // MOBA_BLOCK
    spec = pl.BlockSpec((1, s, LANES), lambda b, hp: (b, 0, hp))
    return pl.pallas_call(
        _moba_prompt_kernel,
        grid=(bsz, ATTN_W // LANES),
        in_specs=[spec, spec, spec, pl.BlockSpec((1, nb, 1, LANES), lambda b, hp: (b, 0, 0, hp))],
        out_specs=spec,
        out_shape=jax.ShapeDtypeStruct((bsz, s, ATTN_W), BF16),
        compiler_params=_cparams(("arbitrary", "arbitrary")),
        name="moba_prompt",
    )(q, k, v, km)


def _moba_sample_kernel(pt_ref, q_ref, kn_ref, vn_ref, ck_ref, cv_ref, o_ref, kbuf, vbuf, sems):
    n_pages = kbuf.shape[1]
    hg = kbuf.shape[2]
    n_groups = pl.num_programs(1)
    seq = pl.program_id(0)
    grp = pl.program_id(1)
    step = seq * n_groups + grp
    total = pl.num_programs(0) * n_groups
    slot = step % 2

    def fetch(sq, gp, sl):
        def body(p, c):
            page = pt_ref[sq * n_pages + p]
            heads = pl.ds(gp * hg, hg)
            pltpu.make_async_copy(ck_ref.at[page, heads], kbuf.at[sl, p], sems.at[0, sl]).start()
            pltpu.make_async_copy(cv_ref.at[page, heads], vbuf.at[sl, p], sems.at[1, sl]).start()
            return c
        lax.fori_loop(0, n_pages, body, 0, unroll=4)

    @pl.when(step == 0)
    def _():
        fetch(seq, grp, slot)

    @pl.when(step + 1 < total)
    def _():
        nxt = step + 1
        fetch(nxt // n_groups, nxt % n_groups, 1 - slot)

    pltpu.make_async_copy(ck_ref.at[pl.ds(0, n_pages), pl.ds(0, hg)], kbuf.at[slot], sems.at[0, slot]).wait()
    pltpu.make_async_copy(cv_ref.at[pl.ds(0, n_pages), pl.ds(0, hg)], vbuf.at[slot], sems.at[1, slot]).wait()

    ppb = MOBA_BLOCK // PAGE_SIZE
    nfp = n_pages // ppb
    n_sel = min(MOBA_TOP, nfp)
    t = q_ref.shape[2]
    blk_id = lax.broadcasted_iota(I32, (nfp, 1, 1), 0).astype(F32)
    page_blk = (lax.broadcasted_iota(I32, (n_pages, 1, 1), 0) // ppb).astype(F32)
    causal = lax.broadcasted_iota(I32, (t, t), 1) <= lax.broadcasted_iota(I32, (t, t), 0)
    batch_nn = (((2,), (1,)), ((0,), (0,)))
    batch_nt = (((2,), (2,)), ((0,), (0,)))

    for hh in range(hg):
        qb = q_ref[0, hh].astype(BF16)
        knb = kn_ref[0, hh].astype(BF16)
        vnb = vn_ref[0, hh].astype(BF16)
        kt = kbuf[slot, :, hh].astype(BF16)
        vt = vbuf[slot, :, hh].astype(BF16)
        s = lax.dot_general(jnp.broadcast_to(qb[None], (n_pages, t, HEAD_DIM)), kt, batch_nn,
                            preferred_element_type=F32)

        score = jnp.sum(jnp.sum(s.reshape(nfp, ppb, t, PAGE_SIZE), axis=1), axis=2, keepdims=True)
        keep = None
        for _ in range(n_sel):
            best = jnp.max(score, axis=0, keepdims=True)
            idx = jnp.min(jnp.where(score == best, blk_id, 1e9), axis=0, keepdims=True)
            score = jnp.where(blk_id == idx, REMOVED, score)
            hit = page_blk == idx
            keep = hit if keep is None else keep | hit

        s = jnp.where(keep, s, NEG_INF)
        so = jnp.where(causal, _dot_nt(qb, knb), NEG_INF)
        m = jnp.maximum(jnp.max(jnp.max(s, axis=0), axis=1, keepdims=True), jnp.max(so, axis=1, keepdims=True))
        p = jnp.exp(s - m[None])
        po = jnp.exp(so - m)
        denom = jnp.sum(jnp.sum(p, axis=0), axis=1, keepdims=True) + jnp.sum(po, axis=1, keepdims=True)
        pv = lax.dot_general(p.astype(BF16), vt, batch_nt, preferred_element_type=F32)
        o_ref[0, hh] = (jnp.sum(pv, axis=0) + _dot(po.astype(BF16), vnb)) / denom


def _moba_sample(q, kn, vn, cache_kt, cache_vt, page_table):
    db, nh, t, _ = q.shape
    n_pages = page_table.shape[1]
    hg = SAMPLE_HEADS
    assert (n_pages * PAGE_SIZE) % MOBA_BLOCK == 0 and n_pages * PAGE_SIZE >= MOBA_BLOCK and nh % hg == 0
    new = pl.BlockSpec((1, hg, t, HEAD_DIM), lambda s, g, pt: (s, g, 0, 0))
    return pl.pallas_call(
        _moba_sample_kernel,
        grid_spec=pltpu.PrefetchScalarGridSpec(
            num_scalar_prefetch=1,
            grid=(db, nh // hg),
            in_specs=[new, new, new, pl.BlockSpec(memory_space=pl.ANY), pl.BlockSpec(memory_space=pl.ANY)],
            out_specs=new,
            scratch_shapes=[pltpu.VMEM((2, n_pages, hg, HEAD_DIM, PAGE_SIZE), F32),
                            pltpu.VMEM((2, n_pages, hg, HEAD_DIM, PAGE_SIZE), F32),
                            pltpu.SemaphoreType.DMA((2, 2))]),
        out_shape=jax.ShapeDtypeStruct((db, nh, t, HEAD_DIM), F32),
        compiler_params=_cparams(("arbitrary", "arbitrary")),
        name="moba_sample",
    )(page_table.reshape(-1), q, kn, vn, cache_kt, cache_vt)


def _conv_tail(y, lng_ref, lnb_ref):
    mu = jnp.mean(y, axis=-1, keepdims=True)
    yc = y - mu
    var = jnp.mean(yc * yc, axis=-1, keepdims=True)
    z = yc * lax.rsqrt(var + LN_EPS) * lng_ref[...] + lnb_ref[...]
    return z * jax.nn.sigmoid(z)


def _post_mix_tail(attn, conv, x, gt1, sh2, sc2, wo_ref, bo_ref, gp1_ref, gp2_ref, rw_ref, rb_ref,
                   cnt_ref, x1_ref, h2_ref, meta_ref, gate_ref):
    tm = x.shape[0]
    half = attn.shape[1]
    mixed = _dot(attn, wo_ref[:half, :]) + _dot(conv.astype(BF16), wo_ref[half:, :]) + bo_ref[...]
    x1 = x + gt1 * _rms(mixed, gp1_ref[...])
    h2 = _rms(x1, gp2_ref[...]) * (1.0 + sc2) + sh2
    x1_ref[...] = x1.reshape(x1_ref.shape)
    _store_rows_as_tiles(h2_ref, h2)

    lane = lax.broadcasted_iota(I32, (1, LANES), 1)
    lane_f = lane.astype(F32)
    h_hi, h_lo = _split(h2)
    w_hi, w_lo = _split(rw_ref[...])
    logits = _dot(h_hi, w_hi) + _dot(h_hi, w_lo) + _dot(h_lo, w_hi) + rb_ref[...]
    logits = jnp.where(lane < N_EXPERTS, logits, REMOVED)
    picks = _top_n(logits, lane_f, TOP_K)
    exps = [jnp.exp(val - picks[0][0]) for val, _ in picks]
    total = exps[0]
    for e in exps[1:]:
        total = total + e
    sel = jnp.zeros((tm, LANES), F32)
    for _, idx in picks:
        sel = jnp.where(lane_f == idx, 1.0, sel)

    before = lax.broadcasted_iota(I32, (tm, tm), 1) < lax.broadcasted_iota(I32, (tm, tm), 0)
    rank_all = cnt_ref[...] + _dot(jnp.where(before, 1.0, 0.0).astype(BF16), sel.astype(BF16))
    cnt_ref[...] = cnt_ref[...] + jnp.sum(sel, axis=0, keepdims=True)

    meta = jnp.zeros((tm, LANES), F32)
    gates = jnp.zeros((tm, LANES), F32)
    for kk, (_, idx) in enumerate(picks):
        rank = jnp.sum(jnp.where(lane_f == idx, rank_all, 0.0), axis=1, keepdims=True)
        meta = jnp.where(lane == kk, idx, meta)
        meta = jnp.where(lane == TOP_K + kk, rank, meta)
        gates = jnp.where(lane == kk, exps[kk] / total, gates)
    meta_ref[...] = meta.astype(I32).reshape(meta_ref.shape)
    gate_ref[...] = gates.reshape(gate_ref.shape)


def _post_mix_prompt_kernel(attn_ref, u_ref, halo_ref, x_ref, gt1_ref, sh2_ref, sc2_ref, cw_ref, cb_ref,
                            lng_ref, lnb_ref, wo_ref, bo_ref, gp1_ref, gp2_ref, rw_ref, rb_ref, cnt0_ref,
                            x1_ref, h2_ref, meta_ref, gate_ref, cnt_ref, ext):
    b = pl.program_id(0)
    i = pl.program_id(1)
    tm = u_ref.shape[1]

    @pl.when((b == 0) & (i == 0))
    def _():
        cnt_ref[...] = cnt0_ref[...]

    ext[0:HALO, :] = jnp.where(i > 0, halo_ref[0], 0.0)
    ext[HALO:HALO + tm, :] = u_ref[0]
    first = HALO - (CONV_TAPS - 1)
    y = cb_ref[...] + cw_ref[0:1, :] * ext[first:first + tm, :]
    for j in range(1, CONV_TAPS):
        y = y + cw_ref[j:j + 1, :] * ext[first + j:first + j + tm, :]
    conv = _conv_tail(y, lng_ref, lnb_ref)
    _post_mix_tail(attn_ref[0], conv, x_ref[0], gt1_ref[0], sh2_ref[0], sc2_ref[0], wo_ref, bo_ref,
                   gp1_ref, gp2_ref, rw_ref, rb_ref, cnt_ref, x1_ref, h2_ref, meta_ref, gate_ref)


def _post_mix_sample_kernel(attn_ref, u_ref, st_ref, x_ref, gt1_ref, sh2_ref, sc2_ref, cw_ref, cb_ref,
                            lng_ref, lnb_ref, wo_ref, bo_ref, gp1_ref, gp2_ref, rw_ref, rb_ref, cnt0_ref,
                            x1_ref, h2_ref, meta_ref, gate_ref, cnt_ref, st_out_ref, ext):
    i = pl.program_id(1)
    nseq, keep, c = st_ref.shape
    t = u_ref.shape[1] // nseq

    @pl.when(i == 0)
    def _():
        cnt_ref[...] = cnt0_ref[...]

    ext[:, 0:keep, :] = st_ref[...]
    ext[:, keep:keep + t, :] = u_ref[0].reshape(nseq, t, c)
    y = cb_ref[...] + cw_ref[0:1, :] * ext[:, 0:t, :]
    for j in range(1, CONV_TAPS):
        y = y + cw_ref[j:j + 1, :] * ext[:, j:j + t, :]
    st_out_ref[...] = ext[:, t:t + keep, :]
    conv = _conv_tail(y.reshape(nseq * t, c), lng_ref, lnb_ref)
    _post_mix_tail(attn_ref[0], conv, x_ref[0], gt1_ref[0], sh2_ref[0], sc2_ref[0], wo_ref, bo_ref,
                   gp1_ref, gp2_ref, rw_ref, rb_ref, cnt_ref, x1_ref, h2_ref, meta_ref, gate_ref)


def _post_mix(prompt, attn, u, prefix, x, mod, cnt0, cw, cb, lng, lnb, wo_bf, bo, gp1, gp2, rw, rb):
    bsz, s, d = x.shape
    c = u.shape[2]
    tm = 256 if prompt else 128
    row = lambda w: pl.BlockSpec((1, tm, w), lambda b, i: (b, i, 0))
    full = lambda a: pl.BlockSpec(a.shape, lambda b, i: (0,) * a.ndim)
    weights = (cw, cb, lng, lnb, wo_bf, bo, gp1, gp2, rw, rb, cnt0)
    mods = [_mod_spec(not prompt, tm, d, ch) for ch in (2, 3, 4)]
    assert d == SUBLANES * LANES, "a token row of the MoE input must fill exactly one (8,128) tile"
    per_b = s // tm
    tiles = pl.BlockSpec((tm * SUBLANES, LANES), lambda b, i: (b * per_b + i, 0))
    out_specs = [row(d), tiles, row(LANES), row(LANES), pl.BlockSpec((1, LANES), lambda b, i: (0, 0))]
    out_shape = [jax.ShapeDtypeStruct((bsz, s, d), F32), jax.ShapeDtypeStruct((bsz * s * SUBLANES, LANES), F32),
                 jax.ShapeDtypeStruct((bsz, s, LANES), I32), jax.ShapeDtypeStruct((bsz, s, LANES), F32),
                 jax.ShapeDtypeStruct((1, LANES), F32)]
    if prompt:
        kern = _post_mix_prompt_kernel
        per = tm // HALO
        side = pl.BlockSpec((1, HALO, c), lambda b, i: (b, jnp.maximum(i * per - 1, 0), 0))
        side_arg = u
        scratch = [pltpu.VMEM((HALO + tm, c), F32)]
    else:
        kern = _post_mix_sample_kernel
        nseq, keep = prefix.shape[0], prefix.shape[1]
        t = s // nseq
        per = tm // t
        side = pl.BlockSpec((per, keep, c), lambda b, i: (i, 0, 0))
        side_arg = prefix
        out_specs.append(pl.BlockSpec((per, keep, c), lambda b, i: (i, 0, 0)))
        out_shape.append(jax.ShapeDtypeStruct(prefix.shape, F32))
        scratch = [pltpu.VMEM((per, -(-(keep + t) // SUBLANES) * SUBLANES, c), F32)]
    return pl.pallas_call(
        kern,
        grid=(bsz, s // tm),
        in_specs=[row(ATTN_W), row(c), side, row(d)] + mods + [full(a) for a in weights],
        out_specs=out_specs,
        out_shape=out_shape,
        scratch_shapes=scratch,
        compiler_params=_cparams(("arbitrary", "arbitrary")),
        name="post_mix_prompt" if prompt else "post_mix_sample",
    )(attn, u, side_arg, x, mod, mod, mod, *weights)


def _meta_spec(index):
    return pl.BlockSpec((1, 1, TOKEN_TILE * META_W), index, memory_space=pltpu.SMEM)


def _tile_rows(ref, row):
    return ref.at[pl.ds(pl.multiple_of(row * SUBLANES, SUBLANES), SUBLANES)]


def _moe_scatter_kernel(cnt_ref, pst_ref, meta_ref, hp_ref, hs_ref, xr_ref, sem, *, tiles_p, n_pad):
    i = pl.program_id(0)
    n_rows = xr_ref.shape[0] // SUBLANES

    def wait_tiles(n):
        for _ in range(n):
            pltpu.make_async_copy(hp_ref, xr_ref.at[pl.ds(0, TOKEN_TILE * SUBLANES)], sem).wait()

    @pl.when(i == 0)
    def _():
        def fill(lo, hi):
            def body(r, c):
                pltpu.make_async_copy(_tile_rows(hp_ref, 0), _tile_rows(xr_ref, r), sem).start()
                return c
            lax.fori_loop(lo, hi, body, 0)

        end = 0
        for e in range(N_EXPERTS):
            used = pst_ref[e] + cnt_ref[e]
            end = pst_ref[e] + (cnt_ref[e] + ROW_BLOCK - 1) // ROW_BLOCK * ROW_BLOCK
            fill(used, end)
        fill(end, n_rows)
        wait_tiles(n_pad // TOKEN_TILE)

    def scatter_tile(src_ref):
        def body(r, c):
            for kk in range(TOP_K):
                dst = pst_ref[meta_ref[0, 0, r * META_W + kk]] + meta_ref[0, 0, r * META_W + TOP_K + kk]
                pltpu.make_async_copy(_tile_rows(src_ref, r), _tile_rows(xr_ref, dst), sem).start()
            return c
        lax.fori_loop(0, TOKEN_TILE, body, 0, unroll=2)
        wait_tiles(TOP_K)

    @pl.when(i < tiles_p)
    def _():
        scatter_tile(hp_ref)

    @pl.when(i >= tiles_p)
    def _():
        scatter_tile(hs_ref)


def _moe_scatter(counts, pstart, meta_tiles, h_p, h_s, n_rows):
    rows_t = TOKEN_TILE * SUBLANES
    tiles_p = h_p.shape[0] // rows_t
    n_tok = (h_p.shape[0] + h_s.shape[0]) // SUBLANES
    n_pad = n_rows - n_tok * TOP_K
    assert h_p.shape[0] % rows_t == 0 and h_s.shape[0] % rows_t == 0 and n_pad % TOKEN_TILE == 0
    return pl.pallas_call(
        functools.partial(_moe_scatter_kernel, tiles_p=tiles_p, n_pad=n_pad),
        grid_spec=pltpu.PrefetchScalarGridSpec(
            num_scalar_prefetch=2,
            grid=(meta_tiles.shape[0],),
            in_specs=[_meta_spec(lambda i, cnt, pst: (i, 0, 0)),
                      pl.BlockSpec((rows_t, LANES), lambda i, cnt, pst: (jnp.minimum(i, tiles_p - 1), 0)),
                      pl.BlockSpec((rows_t, LANES), lambda i, cnt, pst: (jnp.maximum(i - tiles_p, 0), 0))],
            out_specs=pl.BlockSpec(memory_space=pl.ANY),
            scratch_shapes=[pltpu.SemaphoreType.DMA(())]),
        out_shape=jax.ShapeDtypeStruct((n_rows * SUBLANES, LANES), F32),
        compiler_params=_cparams(("arbitrary",)),
        name="moe_scatter",
    )(counts, pstart, meta_tiles, h_p, h_s)


def _moe_expert_kernel(be_ref, na_ref, x_ref, wgu_ref, bgu_ref, wd_ref, bd_ref, y_ref, wgu_bf, wd_bf):
    b = pl.program_id(0)
    active = b < na_ref[0]
    prev = be_ref[jnp.maximum(b - 1, 0)]

    @pl.when(active & ((b == 0) | (be_ref[b] != prev)))
    def _():
        wgu_bf[...] = wgu_ref[0].astype(BF16)
        wd_bf[...] = wd_ref[0].astype(BF16)

    @pl.when(active)
    def _():
        de = wd_bf.shape[0]
        x = _load_rows_from_tiles(x_ref, ROW_BLOCK)
        gu = _dot(x.astype(BF16), wgu_bf[...]) + bgu_ref[0]
        glu = jnp.minimum(gu[:, :de], SWIGLU_LIMIT)
        lin = jnp.clip(gu[:, de:], -SWIGLU_LIMIT, SWIGLU_LIMIT)
        act = glu * jax.nn.sigmoid(SWIGLU_ALPHA * glu) * (lin + 1.0)
        _store_rows_as_tiles(y_ref, _dot(act.astype(BF16), wd_bf[...]) + bd_ref[0])

    @pl.when(jnp.logical_not(active))
    def _():
        y_ref[...] = jnp.zeros(y_ref.shape, y_ref.dtype)


def _moe_expert(block_e, n_active, xr, wgu, bgu, wd, bd):
    ne, d, two_de = wgu.shape
    de = wd.shape[1]
    nblk = xr.shape[0] // (ROW_BLOCK * SUBLANES)
    tiles = lambda index: pl.BlockSpec((ROW_BLOCK * SUBLANES, LANES), index)
    last = lambda b, na: jnp.minimum(b, na[0] - 1)
    return pl.pallas_call(
        _moe_expert_kernel,
        grid_spec=pltpu.PrefetchScalarGridSpec(
            num_scalar_prefetch=2,
            grid=(nblk,),
            in_specs=[tiles(lambda b, be, na: (last(b, na), 0)),
                      pl.BlockSpec((1, d, two_de), lambda b, be, na: (be[last(b, na)], 0, 0)),
                      pl.BlockSpec((1, 1, two_de), lambda b, be, na: (be[last(b, na)], 0, 0)),
                      pl.BlockSpec((1, de, d), lambda b, be, na: (be[last(b, na)], 0, 0)),
                      pl.BlockSpec((1, 1, d), lambda b, be, na: (be[last(b, na)], 0, 0))],
            out_specs=tiles(lambda b, be, na: (b, 0)),
            scratch_shapes=[pltpu.VMEM((d, two_de), BF16), pltpu.VMEM((de, d), BF16)]),
        out_shape=jax.ShapeDtypeStruct(xr.shape, F32),
        compiler_params=_cparams(("arbitrary",)),
        name="moe_expert",
    )(block_e, n_active, xr, wgu, bgu.reshape(ne, 1, two_de), wd, bd.reshape(ne, 1, d))


def _moe_combine_kernel(pst_ref, meta_ref, nmeta_ref, yr_ref, gate_ref, x1_ref, gt2_ref, g_ref, o_ref, buf, sems):
    step = pl.program_id(0) * pl.num_programs(1) + pl.program_id(1)
    total = pl.num_programs(0) * pl.num_programs(1)
    slot = step % 2
    tm = buf.shape[2] // SUBLANES

    def gather(m_ref, sl):
        def body(r, c):
            for kk in range(TOP_K):
                src = pst_ref[m_ref[0, 0, r * META_W + kk]] + m_ref[0, 0, r * META_W + TOP_K + kk]
                pltpu.make_async_copy(_tile_rows(yr_ref, src), _tile_rows(buf.at[sl, kk], r), sems.at[sl]).start()
            return c
        lax.fori_loop(0, tm, body, 0, unroll=2)

    @pl.when(step == 0)
    def _():
        gather(meta_ref, slot)

    @pl.when(step + 1 < total)
    def _():
        gather(nmeta_ref, 1 - slot)

    for kk in range(TOP_K):
        pltpu.make_async_copy(yr_ref.at[pl.ds(0, tm * SUBLANES)], buf.at[slot, kk], sems.at[slot]).wait()
    gates = gate_ref[0]
    f = gates[:, 0:1] * _load_rows_from_tiles(buf, tm, (slot, 0))
    for kk in range(1, TOP_K):
        f = f + gates[:, kk:kk + 1] * _load_rows_from_tiles(buf, tm, (slot, kk))
    o_ref[0] = x1_ref[0] + gt2_ref[0] * _rms(f, g_ref[...])


def _moe_combine(prompt, pstart, meta_tiles, tile0, yr, gates, x1, mod, g_post2):
    bsz, s, d = x1.shape
    tm = TOKEN_TILE
    per_b = s // tm
    last = tile0 + bsz * per_b - 1
    row = lambda w: pl.BlockSpec((1, tm, w), lambda b, i, pst: (b, i, 0))
    if prompt:
        gt2 = pl.BlockSpec((1, 1, d), lambda b, i, pst: (b, 0, 5))
    else:
        gt2 = pl.BlockSpec((1, tm, d), lambda b, i, pst: (0, i, 5))
    return pl.pallas_call(
        _moe_combine_kernel,
        grid_spec=pltpu.PrefetchScalarGridSpec(
            num_scalar_prefetch=1,
            grid=(bsz, per_b),
            in_specs=[_meta_spec(lambda b, i, pst: (tile0 + b * per_b + i, 0, 0)),
                      _meta_spec(lambda b, i, pst: (jnp.minimum(tile0 + b * per_b + i + 1, last), 0, 0)),
                      pl.BlockSpec(memory_space=pl.ANY),
                      row(LANES), row(d), gt2,
                      pl.BlockSpec((1, d), lambda b, i, pst: (0, 0))],
            out_specs=row(d),
            scratch_shapes=[pltpu.VMEM((2, TOP_K, tm * SUBLANES, LANES), F32), pltpu.SemaphoreType.DMA((2,))]),
        out_shape=jax.ShapeDtypeStruct((bsz, s, d), F32),
        compiler_params=_cparams(("arbitrary", "arbitrary")),
        name="moe_combine_prompt" if prompt else "moe_combine_sample",
    )(pstart, meta_tiles, meta_tiles, yr, gates, x1, mod, g_post2)


def kernel(x_prompt, x_sample, cache_k, cache_v, state_conv, page_table, c_prompt, c_sample, w_mod, b_mod,
           g_pre1, g_post1, g_pre2, g_post2, w_in, conv_w, conv_b, conv_ln_g, conv_ln_b, w_out, b_out,
           router_w, router_b, w_gate_up, b_gate_up, w_down, b_down):
    assert w_mod.shape[0] == 1, "single-layer trunk"
    bsz, s, d = x_prompt.shape
    db, t, _ = x_sample.shape
    c = conv_w.shape[2]
    keep = CONV_TAPS - 1
    rows_s = db * t

    mod = _mod(jnp.concatenate([c_prompt, c_sample], axis=0), w_mod[0], b_mod[0])
    mod_p = mod[:bsz].reshape(bsz, 1, 6 * d)
    mod_s = jnp.repeat(mod[bsz:], t, axis=0).reshape(1, rows_s, 6 * d)

    w_in_bf = w_in[0].astype(BF16)
    wo_bf = w_out[0].astype(BF16)
    cw = jnp.concatenate([conv_w[0], jnp.zeros((HALO - CONV_TAPS, c), F32)], axis=0)
    rw = jnp.concatenate([router_w[0], jnp.zeros((d, LANES - N_EXPERTS), F32)], axis=1)
    rb = jnp.concatenate([router_b[0], jnp.zeros((LANES - N_EXPERTS,), F32)]).reshape(1, LANES)
    shared = (cw, conv_b, conv_ln_g, conv_ln_b, wo_bf, b_out, g_post1, g_pre2, rw, rb)

    q_p, kb_p, vb_p, k_pages, v_pages, u_p, km_p = _in_proj_prompt(x_prompt, mod_p, g_pre1, w_in_bf)
    q_s, k_s, v_s, u_s = _in_proj_sample(x_sample, mod_s, g_pre1, w_in_bf)

    attn_p = _moba_prompt(q_p, kb_p, vb_p, km_p)
    attn_s = _moba_sample(q_s, k_s, v_s, jnp.swapaxes(cache_k[0], 2, 3), jnp.swapaxes(cache_v[0], 2, 3), page_table)
    attn_s = attn_s.transpose(0, 2, 1, 3).reshape(1, rows_s, ATTN_W).astype(BF16)

    cnt0 = jnp.zeros((1, LANES), F32)
    x1_p, h2_p, meta_p, gate_p, cnt_p = _post_mix(True, attn_p, u_p, None, x_prompt, mod_p, cnt0, *shared)
    x1_s, h2_s, meta_s, gate_s, cnt, conv_s = _post_mix(False, attn_s, u_s, state_conv[0],
                                                        x_sample.reshape(1, rows_s, d), mod_s, cnt_p, *shared)

    n_tok = bsz * s + rows_s
    counts = cnt[0, :N_EXPERTS].astype(I32)
    padded = (counts + ROW_BLOCK - 1) // ROW_BLOCK * ROW_BLOCK
    pend = jnp.cumsum(padded)
    pstart = pend - padded
    nblk = -(-(n_tok * TOP_K) // ROW_BLOCK) + N_EXPERTS
    starts = jnp.arange(nblk, dtype=I32) * ROW_BLOCK
    block_e = jnp.minimum(jnp.sum((pend[None, :] <= starts[:, None]).astype(I32), axis=1), N_EXPERTS - 1)
    n_active = (pend[-1:] // ROW_BLOCK).astype(I32)
    meta = jnp.concatenate([meta_p.reshape(bsz * s, LANES)[:, :META_W], meta_s.reshape(rows_s, LANES)[:, :META_W]],
                           axis=0)
    meta_tiles = meta.reshape(n_tok // TOKEN_TILE, 1, TOKEN_TILE * META_W)

    xr = _moe_scatter(counts, pstart, meta_tiles, h2_p, h2_s, nblk * ROW_BLOCK)
    yr = _moe_expert(block_e, n_active, xr, w_gate_up[0], b_gate_up[0], w_down[0], b_down[0])
    y_p = _moe_combine(True, pstart, meta_tiles, 0, yr, gate_p, x1_p, mod_p, g_post2)
    y_s = _moe_combine(False, pstart, meta_tiles, bsz * s // TOKEN_TILE, yr, gate_s, x1_s, mod_s, g_post2)

    conv_p = u_p[:, s - keep:, :]
    return (y_p, y_s.reshape(db, t, d), k_pages[None], v_pages[None], conv_p[None],
            k_s[None], v_s[None], conv_s[None])
```

```python
import functools

import jax
import jax.numpy as jnp
from jax import lax
from jax.experimental import pallas as pl
from jax.experimental.pallas import tpu as pltpu

F32 = jnp.float32
BF16 = jnp.bfloat16
I32 = jnp.int32

ATTN_HEADS = 8
HEAD_DIM = 64
ATTN_W = ATTN_HEADS * HEAD_DIM
CONV_TAPS = 31
MOBA_BLOCK = 256
MOBA_TOP = 3
PAGE_SIZE = 128
N_EXPERTS = 32
TOP_K = 4
SWIGLU_LIMIT = 7.0
SWIGLU_ALPHA = 1.702
RMS_EPS = 1e-6
LN_EPS = 1e-5
NEG_INF = -1e30
REMOVED = -3e38

LANES = 128
SUBLANES = 8
VMEM_LIMIT = 56 * 1024 * 1024

KV_STEP = 4 * MOBA_BLOCK
SAMPLE_HEADS = 4
ROW_BLOCK = 256
TOKEN_TILE = 256
META_W = 2 * TOP_K
HALO = 32


def _cparams(sem):
    return pltpu.CompilerParams(dimension_semantics=sem, vmem_limit_bytes=VMEM_LIMIT)


def _dot(a, b):
    return jnp.dot(a, b, preferred_element_type=F32)


def _dot_nt(a, b):
    return lax.dot_general(a, b, (((1,), (1,)), ((), ())), preferred_element_type=F32)


def _split(x):
    hi = x.astype(BF16)
    lo = (x - hi.astype(F32)).astype(BF16)
    return hi, lo


def _rms(x, g):
    return x * lax.rsqrt(jnp.mean(x * x, axis=-1, keepdims=True) + RMS_EPS) * g


def _store_rows_as_tiles(ref, x):
    rows = x.shape[0]
    for c in range(SUBLANES):
        ref[pl.ds(c, rows, stride=SUBLANES), :] = x[:, c * LANES:(c + 1) * LANES]


def _load_rows_from_tiles(ref, rows, lead=()):
    return jnp.concatenate([ref[(*lead, pl.ds(c, rows, stride=SUBLANES), slice(None))] for c in range(SUBLANES)],
                           axis=1)


def _top_n(score, lane_f, n):
    out = []
    work = score
    for _ in range(n):
        m = jnp.max(work, axis=1, keepdims=True)
        idx = jnp.min(jnp.where(work == m, lane_f, 1e9), axis=1, keepdims=True)
        out.append((m, idx))
        work = jnp.where(lane_f == idx, REMOVED, work)
    return out


def _mod_kernel(c_ref, w_ref, b_ref, o_ref):
    c = c_ref[...]
    s = (c * jax.nn.sigmoid(c)).astype(BF16)
    o_ref[...] = _dot(s, w_ref[...].astype(BF16)) + b_ref[...]


def _mod(c, w, b):
    n, d = c.shape
    dout = w.shape[1]
    bn = 1536
    return pl.pallas_call(
        _mod_kernel,
        grid=(dout // bn,),
        in_specs=[pl.BlockSpec((n, d), lambda j: (0, 0)),
                  pl.BlockSpec((d, bn), lambda j: (0, j)),
                  pl.BlockSpec((1, bn), lambda j: (0, j))],
        out_specs=pl.BlockSpec((n, bn), lambda j: (0, j)),
        out_shape=jax.ShapeDtypeStruct((n, dout), F32),
        compiler_params=_cparams(("arbitrary",)),
        name="mod",
    )(c, w, b.reshape(1, dout))


def _in_proj_kernel(x_ref, sh_ref, sc_ref, g_ref, w_ref, *outs, prompt):
    x = x_ref[0]
    tm = x.shape[0]
    h = _rms(x, g_ref[...]) * (1.0 + sc_ref[0]) + sh_ref[0]
    proj = _dot(h.astype(BF16), w_ref[...])
    q = proj[:, :ATTN_W] * (HEAD_DIM ** -0.5)
    k = proj[:, ATTN_W:2 * ATTN_W]
    v = proj[:, 2 * ATTN_W:3 * ATTN_W]
    c = (proj.shape[1] - 3 * ATTN_W) // 2
    a = proj[:, 3 * ATTN_W:3 * ATTN_W + c]
    g = proj[:, 3 * ATTN_W + c:]
    u = a * jax.nn.sigmoid(g)
    if prompt:
        q_ref, kb_ref, vb_ref, kf_ref, vf_ref, u_ref, km_ref = outs
        q_ref[0] = q.astype(BF16)
        kb_ref[0] = k.astype(BF16)
        vb_ref[0] = v.astype(BF16)
        u_ref[0] = u
        for p in range(tm // PAGE_SIZE):
            rows = slice(p * PAGE_SIZE, (p + 1) * PAGE_SIZE)
            for hh in range(ATTN_HEADS):
                cols = slice(hh * HEAD_DIM, (hh + 1) * HEAD_DIM)
                kf_ref[0, p, hh] = k[rows, cols]
                vf_ref[0, p, hh] = v[rows, cols]
        for blk in range(tm // MOBA_BLOCK):
            rows = slice(blk * MOBA_BLOCK, (blk + 1) * MOBA_BLOCK)
            km_ref[0, blk] = jnp.mean(k[rows], axis=0, keepdims=True)
    else:
        q_ref, kf_ref, vf_ref, u_ref = outs
        u_ref[0] = u
        t = q_ref.shape[2]
        for hh in range(ATTN_HEADS):
            cols = slice(hh * HEAD_DIM, (hh + 1) * HEAD_DIM)
            q_ref[:, hh] = q[:, cols].reshape(tm // t, t, HEAD_DIM)
            kf_ref[:, hh] = k[:, cols].reshape(tm // t, t, HEAD_DIM)
            vf_ref[:, hh] = v[:, cols].reshape(tm // t, t, HEAD_DIM)


def _mod_spec(per_row, tm, d, chunk):
    if per_row:
        return pl.BlockSpec((1, tm, d), lambda b, i: (0, i, chunk))
    return pl.BlockSpec((1, 1, d), lambda b, i: (b, 0, chunk))


def _in_proj_prompt(x, mod, g_pre1, w_in_bf):
    bsz, s, d = x.shape
    tm = 512
    c = (w_in_bf.shape[1] - 3 * ATTN_W) // 2
    n_pages = s // PAGE_SIZE
    nb = s // MOBA_BLOCK
    row = lambda w: pl.BlockSpec((1, tm, w), lambda b, i: (b, i, 0))
    paged = pl.BlockSpec((1, tm // PAGE_SIZE, ATTN_HEADS, PAGE_SIZE, HEAD_DIM), lambda b, i: (b, i, 0, 0, 0))
    return pl.pallas_call(
        functools.partial(_in_proj_kernel, prompt=True),
        grid=(bsz, s // tm),
        in_specs=[row(d), _mod_spec(False, tm, d, 0), _mod_spec(False, tm, d, 1),
                  pl.BlockSpec((1, d), lambda b, i: (0, 0)),
                  pl.BlockSpec(w_in_bf.shape, lambda b, i: (0, 0))],
        out_specs=[row(ATTN_W), row(ATTN_W), row(ATTN_W), paged, paged, row(c),
                   pl.BlockSpec((1, tm // MOBA_BLOCK, 1, ATTN_W), lambda b, i: (b, i, 0, 0))],
        out_shape=[jax.ShapeDtypeStruct((bsz, s, ATTN_W), BF16)] * 3
        + [jax.ShapeDtypeStruct((bsz, n_pages, ATTN_HEADS, PAGE_SIZE, HEAD_DIM), F32)] * 2
        + [jax.ShapeDtypeStruct((bsz, s, c), F32),
           jax.ShapeDtypeStruct((bsz, nb, 1, ATTN_W), F32)],
        compiler_params=_cparams(("arbitrary", "arbitrary")),
        name="in_proj_prompt",
    )(x, mod, mod, g_pre1, w_in_bf)


def _in_proj_sample(x, mod_rows, g_pre1, w_in_bf):
    db, t, d = x.shape
    rows = db * t
    tm = 128
    c = (w_in_bf.shape[1] - 3 * ATTN_W) // 2
    heads = pl.BlockSpec((tm // t, ATTN_HEADS, t, HEAD_DIM), lambda b, i: (i, 0, 0, 0))
    return pl.pallas_call(
        functools.partial(_in_proj_kernel, prompt=False),
        grid=(1, rows // tm),
        in_specs=[pl.BlockSpec((1, tm, d), lambda b, i: (0, i, 0)),
                  _mod_spec(True, tm, d, 0), _mod_spec(True, tm, d, 1),
                  pl.BlockSpec((1, d), lambda b, i: (0, 0)),
                  pl.BlockSpec(w_in_bf.shape, lambda b, i: (0, 0))],
        out_specs=[heads, heads, heads, pl.BlockSpec((1, tm, c), lambda b, i: (0, i, 0))],
        out_shape=[jax.ShapeDtypeStruct((db, ATTN_HEADS, t, HEAD_DIM), F32)] * 3
        + [jax.ShapeDtypeStruct((1, rows, c), F32)],
        compiler_params=_cparams(("arbitrary", "arbitrary")),
        name="in_proj_sample",
    )(x.reshape(1, rows, d), mod_rows, mod_rows, g_pre1, w_in_bf)


def _moba_prompt_kernel(q_ref, k_ref, v_ref, km_ref, o_ref, kh_scr, vh_scr, s0_scr):
    s_len = q_ref.shape[1]
    nb = s_len // MOBA_BLOCK
    lane = lax.broadcasted_iota(I32, (1, LANES), 1)
    lane_f = lane.astype(F32)
    row_i = lax.broadcasted_iota(I32, (MOBA_BLOCK, MOBA_BLOCK), 0)
    col_i = lax.broadcasted_iota(I32, (MOBA_BLOCK, MOBA_BLOCK), 1)
    causal = col_i <= row_i
    per = KV_STEP // MOBA_BLOCK
    km = km_ref[0, :, 0, :]

    heads = []
    for hh in range(2):
        in_head = (lane >= hh * HEAD_DIM) & (lane < (hh + 1) * HEAD_DIM)
        off = HEAD_DIM * (1 - hh)
        km_h = jnp.where(in_head, km, 0.0)
        parts = []
        if off:
            parts.append(jnp.zeros((off, LANES), F32))
        parts.append(km_h)
        parts.append(jnp.zeros((LANES - off - nb, LANES), F32))
        km_hi, km_lo = _split(jnp.concatenate(parts, axis=0))
        heads.append((in_head, off, km_hi, km_lo))

    def recycle(j, carry):
        r0 = pl.multiple_of(j * MOBA_BLOCK, MOBA_BLOCK)
        kj = k_ref[0, pl.ds(r0, MOBA_BLOCK), :]
        vj = v_ref[0, pl.ds(r0, MOBA_BLOCK), :]
        for hh, (in_head, off, _, _) in enumerate(heads):
            onehot = jnp.where(lane == off + j, 1.0, 0.0).astype(BF16)
            kh_scr[hh, pl.ds(r0, MOBA_BLOCK), :] = jnp.where(in_head, kj, onehot)
            vh_scr[hh, pl.ds(r0, MOBA_BLOCK), :] = jnp.where(in_head, vj, jnp.ones_like(vj))
        return carry

    lax.fori_loop(0, nb, recycle, 0)

    def q_tile(i, carry):
        q0 = pl.multiple_of(i * MOBA_BLOCK, MOBA_BLOCK)
        q = q_ref[0, pl.ds(q0, MOBA_BLOCK), :]
        kd = k_ref[0, pl.ds(q0, MOBA_BLOCK), :]
        vd = v_ref[0, pl.ds(q0, MOBA_BLOCK), :]
        q_augs, state = [], []
        for in_head, off, km_hi, km_lo in heads:
            qh = jnp.where(in_head, q, jnp.zeros_like(q))
            blk = lane - off
            in_bias = (blk >= 0) & (blk < nb)
            valid = in_bias & (blk < i)
            score = jnp.where(valid, _dot_nt(qh, km_hi) + _dot_nt(qh, km_lo), NEG_INF)
            sel = jnp.zeros(score.shape, F32)
            for _, idx in _top_n(score, lane_f, MOBA_TOP):
                sel = jnp.where((lane_f == idx) & valid, 1.0, sel)
            bias = jnp.where(in_bias & (sel == 0.0), NEG_INF, 0.0).astype(BF16)
            q_augs.append(jnp.where(in_head, q, bias))

            s = jnp.where(causal, _dot_nt(qh, kd), NEG_INF)
            m = jnp.max(s, axis=1, keepdims=True)
            p = jnp.exp(s - m)
            state.append((m, _dot(p.astype(BF16), jnp.where(in_head, vd, jnp.ones_like(vd)))))

        def scores(hh, jj):
            k0 = pl.multiple_of(jnp.minimum(jj * KV_STEP, s_len - KV_STEP), KV_STEP)
            return _dot_nt(q_augs[hh], kh_scr[hh, pl.ds(k0, KV_STEP), :]).astype(BF16)

        def absorb(hh, jj, sb, m, acc):
            k0 = pl.multiple_of(jj * KV_STEP, KV_STEP)
            tile_max = sb[:, :LANES]
            for cc in range(1, KV_STEP // LANES):
                tile_max = jnp.maximum(tile_max, sb[:, cc * LANES:(cc + 1) * LANES])
            m_new = jnp.maximum(m, jnp.max(tile_max.astype(F32), axis=1, keepdims=True))
            alpha = jnp.exp(m - m_new)
            p = jnp.exp(sb - m_new.astype(BF16))
            return m_new, alpha * acc + _dot(p, vh_scr[hh, pl.ds(k0, KV_STEP), :])

        s0_scr[...] = scores(0, 0)

        def kv_step(jj, st):
            sb1 = scores(1, jj)
            first = absorb(0, jj, s0_scr[...], *st[0])
            second = absorb(1, jj, sb1, *st[1])
            s0_scr[...] = scores(0, jj + 1)
            return first, second

        state = lax.fori_loop(0, (i + per - 1) // per, kv_step, tuple(state))
        outs = [acc / pltpu.roll(acc, HEAD_DIM, axis=1) for _, acc in state]
        o_ref[0, pl.ds(q0, MOBA_BLOCK), :] = jnp.where(heads[0][0], outs[0], outs[1]).astype(o_ref.dtype)
        return carry

    lax.fori_loop(0, nb, q_tile, 0)


def _moba_prompt(q, k, v, km):
    bsz, s, _ = q.shape
    assert 2 * HEAD_DIM == LANES and s % KV_STEP == 0 and s // MOBA_BLOCK <= HEAD_DIM
    nb = s // MOBA_BLOCK
    spec = pl.BlockSpec((1, s, LANES), lambda b, hp: (b, 0, hp))
    return pl.pallas_call(
        _moba_prompt_kernel,
        grid=(bsz, ATTN_W // LANES),
        in_specs=[spec, spec, spec, pl.BlockSpec((1, nb, 1, LANES), lambda b, hp: (b, 0, 0, hp))],
        out_specs=spec,
        out_shape=jax.ShapeDtypeStruct((bsz, s, ATTN_W), BF16),
        scratch_shapes=[pltpu.VMEM((2, s, LANES), BF16), pltpu.VMEM((2, s, LANES), BF16),
                        pltpu.VMEM((MOBA_BLOCK, KV_STEP), BF16)],
        compiler_params=_cparams(("arbitrary", "arbitrary")),
        name="moba_prompt",
    )(q, k, v, km)


def _moba_sample_kernel(pt_ref, q_ref, kn_ref, vn_ref, ck_ref, cv_ref, o_ref, kbuf, vbuf, sems):
    n_pages = kbuf.shape[1]
    hg = kbuf.shape[2]
    n_groups = pl.num_programs(1)
    seq = pl.program_id(0)
    grp = pl.program_id(1)
    step = seq * n_groups + grp
    total = pl.num_programs(0) * n_groups
    slot = step % 2

    def fetch(sq, gp, sl):
        def body(p, c):
            page = pt_ref[sq * n_pages + p]
            heads = pl.ds(gp * hg, hg)
            pltpu.make_async_copy(ck_ref.at[page, heads], kbuf.at[sl, p], sems.at[0, sl]).start()
            pltpu.make_async_copy(cv_ref.at[page, heads], vbuf.at[sl, p], sems.at[1, sl]).start()
            return c
        lax.fori_loop(0, n_pages, body, 0, unroll=4)

    @pl.when(step == 0)
    def _():
        fetch(seq, grp, slot)

    @pl.when(step + 1 < total)
    def _():
        nxt = step + 1
        fetch(nxt // n_groups, nxt % n_groups, 1 - slot)

    pltpu.make_async_copy(ck_ref.at[pl.ds(0, n_pages), pl.ds(0, hg)], kbuf.at[slot], sems.at[0, slot]).wait()
    pltpu.make_async_copy(cv_ref.at[pl.ds(0, n_pages), pl.ds(0, hg)], vbuf.at[slot], sems.at[1, slot]).wait()

    ppb = MOBA_BLOCK // PAGE_SIZE
    nfp = n_pages // ppb
    n_sel = min(MOBA_TOP, nfp)
    t = q_ref.shape[2]
    rows = hg * t
    width = hg * HEAD_DIM
    blk_id = lax.broadcasted_iota(I32, (nfp, 1, 1), 0).astype(F32)
    page_blk = (lax.broadcasted_iota(I32, (n_pages, 1, 1), 0) // ppb).astype(F32)
    row_head = lax.broadcasted_iota(I32, (rows, 1), 0) // t
    own = row_head == lax.broadcasted_iota(I32, (1, width), 1) // HEAD_DIM
    tile = (lax.broadcasted_iota(I32, (HEAD_DIM, width), 1) % HEAD_DIM
            == lax.broadcasted_iota(I32, (HEAD_DIM, width), 0)).astype(BF16)
    r_i = lax.broadcasted_iota(I32, (rows, rows), 0)
    c_i = lax.broadcasted_iota(I32, (rows, rows), 1)
    causal = (r_i // t == c_i // t) & (c_i % t <= r_i % t)
    batch_nn = (((2,), (1,)), ((0,), (0,)))
    batch_nt = (((2,), (2,)), ((0,), (0,)))

    def block_diag(x):
        spread = _dot(x.reshape(rows, HEAD_DIM).astype(BF16), tile)
        return jnp.where(own, spread, 0.0).astype(BF16)

    q_bd = block_diag(q_ref[0])
    kn_bd = block_diag(kn_ref[0])
    vnb = vn_ref[0].reshape(rows, HEAD_DIM).astype(BF16)
    kt = kbuf[slot].reshape(n_pages, width, PAGE_SIZE).astype(BF16)
    vt = vbuf[slot].reshape(n_pages, width, PAGE_SIZE).astype(BF16)
    s = lax.dot_general(jnp.broadcast_to(q_bd[None], (n_pages, rows, width)), kt, batch_nn,
                        preferred_element_type=F32)

    score = jnp.sum(jnp.sum(s.reshape(nfp, ppb, rows, PAGE_SIZE), axis=1), axis=2, keepdims=True)
    keep = None
    for _ in range(n_sel):
        best = jnp.max(score, axis=0, keepdims=True)
        idx = jnp.min(jnp.where(score == best, blk_id, 1e9), axis=0, keepdims=True)
        score = jnp.where(blk_id == idx, REMOVED, score)
        hit = page_blk == idx
        keep = hit if keep is None else keep | hit

    s = jnp.where(keep, s, NEG_INF)
    so = jnp.where(causal, _dot_nt(q_bd, kn_bd), NEG_INF)
    m = jnp.maximum(jnp.max(jnp.max(s, axis=0), axis=1, keepdims=True), jnp.max(so, axis=1, keepdims=True))
    p = jnp.exp(s - m[None])
    po = jnp.exp(so - m)
    denom = jnp.sum(jnp.sum(p, axis=0), axis=1, keepdims=True) + jnp.sum(po, axis=1, keepdims=True)
    pv = jnp.sum(lax.dot_general(p.astype(BF16), vt, batch_nt, preferred_element_type=F32), axis=0)
    past = jnp.zeros((rows, HEAD_DIM), F32)
    for hh in range(hg):
        past = past + jnp.where(row_head == hh, pv[:, hh * HEAD_DIM:(hh + 1) * HEAD_DIM], 0.0)
    o_ref[0] = ((past + _dot(po.astype(BF16), vnb)) / denom).reshape(hg, t, HEAD_DIM)


def _moba_sample(q, kn, vn, cache_kt, cache_vt, page_table):
    db, nh, t, _ = q.shape
    n_pages = page_table.shape[1]
    hg = SAMPLE_HEADS
    assert (n_pages * PAGE_SIZE) % MOBA_BLOCK == 0 and n_pages * PAGE_SIZE >= MOBA_BLOCK and nh % hg == 0
    new = pl.BlockSpec((1, hg, t, HEAD_DIM), lambda s, g, pt: (s, g, 0, 0))
    return pl.pallas_call(
        _moba_sample_kernel,
        grid_spec=pltpu.PrefetchScalarGridSpec(
            num_scalar_prefetch=1,
            grid=(db, nh // hg),
            in_specs=[new, new, new, pl.BlockSpec(memory_space=pl.ANY), pl.BlockSpec(memory_space=pl.ANY)],
            out_specs=new,
            scratch_shapes=[pltpu.VMEM((2, n_pages, hg, HEAD_DIM, PAGE_SIZE), F32),
                            pltpu.VMEM((2, n_pages, hg, HEAD_DIM, PAGE_SIZE), F32),
                            pltpu.SemaphoreType.DMA((2, 2))]),
        out_shape=jax.ShapeDtypeStruct((db, nh, t, HEAD_DIM), F32),
        compiler_params=_cparams(("arbitrary", "arbitrary")),
        name="moba_sample",
    )(page_table.reshape(-1), q, kn, vn, cache_kt, cache_vt)


def _conv_tail(y, lng_ref, lnb_ref):
    mu = jnp.mean(y, axis=-1, keepdims=True)
    yc = y - mu
    var = jnp.mean(yc * yc, axis=-1, keepdims=True)
    z = yc * lax.rsqrt(var + LN_EPS) * lng_ref[...] + lnb_ref[...]
    return z * jax.nn.sigmoid(z)


def _post_mix_tail(attn, conv, x, gt1, sh2, sc2, wo_ref, bo_ref, gp1_ref, gp2_ref, rw_ref, rb_ref,
                   cnt_ref, x1_ref, h2_ref, meta_ref, gate_ref):
    tm = x.shape[0]
    half = attn.shape[1]
    mixed = _dot(attn, wo_ref[:half, :]) + _dot(conv.astype(BF16), wo_ref[half:, :]) + bo_ref[...]
    x1 = x + gt1 * _rms(mixed, gp1_ref[...])
    h2 = _rms(x1, gp2_ref[...]) * (1.0 + sc2) + sh2
    x1_ref[...] = x1.reshape(x1_ref.shape)
    _store_rows_as_tiles(h2_ref, h2)

    lane = lax.broadcasted_iota(I32, (1, LANES), 1)
    lane_f = lane.astype(F32)
    h_hi, h_lo = _split(h2)
    w_hi, w_lo = _split(rw_ref[...])
    logits = _dot(h_hi, w_hi) + _dot(h_hi, w_lo) + _dot(h_lo, w_hi) + rb_ref[...]
    logits = jnp.where(lane < N_EXPERTS, logits, REMOVED)
    picks = _top_n(logits, lane_f, TOP_K)
    exps = [jnp.exp(val - picks[0][0]) for val, _ in picks]
    total = exps[0]
    for e in exps[1:]:
        total = total + e
    sel = jnp.zeros((tm, LANES), F32)
    for _, idx in picks:
        sel = jnp.where(lane_f == idx, 1.0, sel)

    before = lax.broadcasted_iota(I32, (tm, tm), 1) < lax.broadcasted_iota(I32, (tm, tm), 0)
    rank_all = cnt_ref[...] + _dot(jnp.where(before, 1.0, 0.0).astype(BF16), sel.astype(BF16))
    cnt_ref[...] = cnt_ref[...] + jnp.sum(sel, axis=0, keepdims=True)

    meta = jnp.zeros((tm, LANES), F32)
    gates = jnp.zeros((tm, LANES), F32)
    for kk, (_, idx) in enumerate(picks):
        rank = jnp.sum(jnp.where(lane_f == idx, rank_all, 0.0), axis=1, keepdims=True)
        meta = jnp.where(lane == kk, idx, meta)
        meta = jnp.where(lane == TOP_K + kk, rank, meta)
        gates = jnp.where(lane == kk, exps[kk] / total, gates)
    meta_ref[...] = meta.astype(I32).reshape(meta_ref.shape)
    gate_ref[...] = gates.reshape(gate_ref.shape)


def _post_mix_prompt_kernel(attn_ref, u_ref, halo_ref, x_ref, gt1_ref, sh2_ref, sc2_ref, cw_ref, cb_ref,
                            lng_ref, lnb_ref, wo_ref, bo_ref, gp1_ref, gp2_ref, rw_ref, rb_ref, cnt0_ref,
                            x1_ref, h2_ref, meta_ref, gate_ref, cnt_ref, ext):
    b = pl.program_id(0)
    i = pl.program_id(1)
    tm = u_ref.shape[1]

    @pl.when((b == 0) & (i == 0))
    def _():
        cnt_ref[...] = cnt0_ref[...]

    ext[0:HALO, :] = jnp.where(i > 0, halo_ref[0], 0.0)
    ext[HALO:HALO + tm, :] = u_ref[0]
    first = HALO - (CONV_TAPS - 1)
    y = cb_ref[...] + cw_ref[0:1, :] * ext[first:first + tm, :]
    for j in range(1, CONV_TAPS):
        y = y + cw_ref[j:j + 1, :] * ext[first + j:first + j + tm, :]
    conv = _conv_tail(y, lng_ref, lnb_ref)
    _post_mix_tail(attn_ref[0], conv, x_ref[0], gt1_ref[0], sh2_ref[0], sc2_ref[0], wo_ref, bo_ref,
                   gp1_ref, gp2_ref, rw_ref, rb_ref, cnt_ref, x1_ref, h2_ref, meta_ref, gate_ref)


def _post_mix_sample_kernel(attn_ref, u_ref, st_ref, x_ref, gt1_ref, sh2_ref, sc2_ref, cw_ref, cb_ref,
                            lng_ref, lnb_ref, wo_ref, bo_ref, gp1_ref, gp2_ref, rw_ref, rb_ref, cnt0_ref,
                            x1_ref, h2_ref, meta_ref, gate_ref, cnt_ref, st_out_ref, ext):
    i = pl.program_id(1)
    nseq, keep, c = st_ref.shape
    t = u_ref.shape[1] // nseq

    @pl.when(i == 0)
    def _():
        cnt_ref[...] = cnt0_ref[...]

    ext[:, 0:keep, :] = st_ref[...]
    ext[:, keep:keep + t, :] = u_ref[0].reshape(nseq, t, c)
    y = cb_ref[...] + cw_ref[0:1, :] * ext[:, 0:t, :]
    for j in range(1, CONV_TAPS):
        y = y + cw_ref[j:j + 1, :] * ext[:, j:j + t, :]
    st_out_ref[...] = ext[:, t:t + keep, :]
    conv = _conv_tail(y.reshape(nseq * t, c), lng_ref, lnb_ref)
    _post_mix_tail(attn_ref[0], conv, x_ref[0], gt1_ref[0], sh2_ref[0], sc2_ref[0], wo_ref, bo_ref,
                   gp1_ref, gp2_ref, rw_ref, rb_ref, cnt_ref, x1_ref, h2_ref, meta_ref, gate_ref)


def _post_mix(prompt, attn, u, prefix, x, mod, cnt0, cw, cb, lng, lnb, wo_bf, bo, gp1, gp2, rw, rb):
    bsz, s, d = x.shape
    c = u.shape[2]
    tm = 256 if prompt else 128
    row = lambda w: pl.BlockSpec((1, tm, w), lambda b, i: (b, i, 0))
    full = lambda a: pl.BlockSpec(a.shape, lambda b, i: (0,) * a.ndim)
    weights = (cw, cb, lng, lnb, wo_bf, bo, gp1, gp2, rw, rb, cnt0)
    mods = [_mod_spec(not prompt, tm, d, ch) for ch in (2, 3, 4)]
    assert d == SUBLANES * LANES, "a token row of the MoE input must fill exactly one (8,128) tile"
    per_b = s // tm
    tiles = pl.BlockSpec((tm * SUBLANES, LANES), lambda b, i: (b * per_b + i, 0))
    out_specs = [row(d), tiles, row(LANES), row(LANES), pl.BlockSpec((1, LANES), lambda b, i: (0, 0))]
    out_shape = [jax.ShapeDtypeStruct((bsz, s, d), F32), jax.ShapeDtypeStruct((bsz * s * SUBLANES, LANES), F32),
                 jax.ShapeDtypeStruct((bsz, s, LANES), I32), jax.ShapeDtypeStruct((bsz, s, LANES), F32),
                 jax.ShapeDtypeStruct((1, LANES), F32)]
    if prompt:
        kern = _post_mix_prompt_kernel
        per = tm // HALO
        side = pl.BlockSpec((1, HALO, c), lambda b, i: (b, jnp.maximum(i * per - 1, 0), 0))
        side_arg = u
        scratch = [pltpu.VMEM((HALO + tm, c), F32)]
    else:
        kern = _post_mix_sample_kernel
        nseq, keep = prefix.shape[0], prefix.shape[1]
        t = s // nseq
        per = tm // t
        side = pl.BlockSpec((per, keep, c), lambda b, i: (i, 0, 0))
        side_arg = prefix
        out_specs.append(pl.BlockSpec((per, keep, c), lambda b, i: (i, 0, 0)))
        out_shape.append(jax.ShapeDtypeStruct(prefix.shape, F32))
        scratch = [pltpu.VMEM((per, -(-(keep + t) // SUBLANES) * SUBLANES, c), F32)]
    return pl.pallas_call(
        kern,
        grid=(bsz, s // tm),
        in_specs=[row(ATTN_W), row(c), side, row(d)] + mods + [full(a) for a in weights],
        out_specs=out_specs,
        out_shape=out_shape,
        scratch_shapes=scratch,
        compiler_params=_cparams(("arbitrary", "arbitrary")),
        name="post_mix_prompt" if prompt else "post_mix_sample",
    )(attn, u, side_arg, x, mod, mod, mod, *weights)


def _meta_spec(index):
    return pl.BlockSpec((1, 1, TOKEN_TILE * META_W), index, memory_space=pltpu.SMEM)


def _tile_rows(ref, row):
    return ref.at[pl.ds(pl.multiple_of(row * SUBLANES, SUBLANES), SUBLANES)]


def _moe_scatter_kernel(cnt_ref, pst_ref, meta_ref, hp_ref, hs_ref, xr_ref, sem, *, tiles_p, n_pad):
    i = pl.program_id(0)
    n_rows = xr_ref.shape[0] // SUBLANES

    def wait_tiles(n):
        for _ in range(n):
            pltpu.make_async_copy(hp_ref, xr_ref.at[pl.ds(0, TOKEN_TILE * SUBLANES)], sem).wait()

    @pl.when(i == 0)
    def _():
        def fill(lo, hi):
            def body(r, c):
                pltpu.make_async_copy(_tile_rows(hp_ref, 0), _tile_rows(xr_ref, r), sem).start()
                return c
            lax.fori_loop(lo, hi, body, 0)

        end = 0
        for e in range(N_EXPERTS):
            used = pst_ref[e] + cnt_ref[e]
            end = pst_ref[e] + (cnt_ref[e] + ROW_BLOCK - 1) // ROW_BLOCK * ROW_BLOCK
            fill(used, end)
        fill(end, n_rows)
        wait_tiles(n_pad // TOKEN_TILE)

    def scatter_tile(src_ref):
        def body(r, c):
            for kk in range(TOP_K):
                dst = pst_ref[meta_ref[0, 0, r * META_W + kk]] + meta_ref[0, 0, r * META_W + TOP_K + kk]
                pltpu.make_async_copy(_tile_rows(src_ref, r), _tile_rows(xr_ref, dst), sem).start()
            return c
        lax.fori_loop(0, TOKEN_TILE, body, 0, unroll=2)
        wait_tiles(TOP_K)

    @pl.when(i < tiles_p)
    def _():
        scatter_tile(hp_ref)

    @pl.when(i >= tiles_p)
    def _():
        scatter_tile(hs_ref)


def _moe_scatter(counts, pstart, meta_tiles, h_p, h_s, n_rows):
    rows_t = TOKEN_TILE * SUBLANES
    tiles_p = h_p.shape[0] // rows_t
    n_tok = (h_p.shape[0] + h_s.shape[0]) // SUBLANES
    n_pad = n_rows - n_tok * TOP_K
    assert h_p.shape[0] % rows_t == 0 and h_s.shape[0] % rows_t == 0 and n_pad % TOKEN_TILE == 0
    return pl.pallas_call(
        functools.partial(_moe_scatter_kernel, tiles_p=tiles_p, n_pad=n_pad),
        grid_spec=pltpu.PrefetchScalarGridSpec(
            num_scalar_prefetch=2,
            grid=(meta_tiles.shape[0],),
            in_specs=[_meta_spec(lambda i, cnt, pst: (i, 0, 0)),
                      pl.BlockSpec((rows_t, LANES), lambda i, cnt, pst: (jnp.minimum(i, tiles_p - 1), 0)),
                      pl.BlockSpec((rows_t, LANES), lambda i, cnt, pst: (jnp.maximum(i - tiles_p, 0), 0))],
            out_specs=pl.BlockSpec(memory_space=pl.ANY),
            scratch_shapes=[pltpu.SemaphoreType.DMA(())]),
        out_shape=jax.ShapeDtypeStruct((n_rows * SUBLANES, LANES), F32),
        compiler_params=_cparams(("arbitrary",)),
        name="moe_scatter",
    )(counts, pstart, meta_tiles, h_p, h_s)


def _moe_expert_kernel(be_ref, na_ref, x_ref, wgu_ref, bgu_ref, wd_ref, bd_ref, y_ref, wgu_bf, wd_bf):
    b = pl.program_id(0)
    active = b < na_ref[0]
    prev = be_ref[jnp.maximum(b - 1, 0)]

    @pl.when(active & ((b == 0) | (be_ref[b] != prev)))
    def _():
        wgu_bf[...] = wgu_ref[0].astype(BF16)
        wd_bf[...] = wd_ref[0].astype(BF16)

    @pl.when(active)
    def _():
        de = wd_bf.shape[0]
        x = _load_rows_from_tiles(x_ref, ROW_BLOCK)
        gu = _dot(x.astype(BF16), wgu_bf[...]) + bgu_ref[0]
        glu = jnp.minimum(gu[:, :de], SWIGLU_LIMIT)
        lin = jnp.clip(gu[:, de:], -SWIGLU_LIMIT, SWIGLU_LIMIT)
        act = glu * jax.nn.sigmoid(SWIGLU_ALPHA * glu) * (lin + 1.0)
        _store_rows_as_tiles(y_ref, _dot(act.astype(BF16), wd_bf[...]) + bd_ref[0])

    @pl.when(jnp.logical_not(active))
    def _():
        y_ref[...] = jnp.zeros(y_ref.shape, y_ref.dtype)


def _moe_expert(block_e, n_active, xr, wgu, bgu, wd, bd):
    ne, d, two_de = wgu.shape
    de = wd.shape[1]
    nblk = xr.shape[0] // (ROW_BLOCK * SUBLANES)
    tiles = lambda index: pl.BlockSpec((ROW_BLOCK * SUBLANES, LANES), index)
    last = lambda b, na: jnp.minimum(b, na[0] - 1)
    return pl.pallas_call(
        _moe_expert_kernel,
        grid_spec=pltpu.PrefetchScalarGridSpec(
            num_scalar_prefetch=2,
            grid=(nblk,),
            in_specs=[tiles(lambda b, be, na: (last(b, na), 0)),
                      pl.BlockSpec((1, d, two_de), lambda b, be, na: (be[last(b, na)], 0, 0)),
                      pl.BlockSpec((1, 1, two_de), lambda b, be, na: (be[last(b, na)], 0, 0)),
                      pl.BlockSpec((1, de, d), lambda b, be, na: (be[last(b, na)], 0, 0)),
                      pl.BlockSpec((1, 1, d), lambda b, be, na: (be[last(b, na)], 0, 0))],
            out_specs=tiles(lambda b, be, na: (b, 0)),
            scratch_shapes=[pltpu.VMEM((d, two_de), BF16), pltpu.VMEM((de, d), BF16)]),
        out_shape=jax.ShapeDtypeStruct(xr.shape, F32),
        compiler_params=_cparams(("arbitrary",)),
        name="moe_expert",
    )(block_e, n_active, xr, wgu, bgu.reshape(ne, 1, two_de), wd, bd.reshape(ne, 1, d))


def _moe_combine_kernel(pst_ref, meta_ref, nmeta_ref, yr_ref, gate_ref, x1_ref, gt2_ref, g_ref, o_ref, buf, sems):
    step = pl.program_id(0) * pl.num_programs(1) + pl.program_id(1)
    total = pl.num_programs(0) * pl.num_programs(1)
    slot = step % 2
    tm = buf.shape[2] // SUBLANES

    def gather(m_ref, sl):
        def body(r, c):
            for kk in range(TOP_K):
                src = pst_ref[m_ref[0, 0, r * META_W + kk]] + m_ref[0, 0, r * META_W + TOP_K + kk]
                pltpu.make_async_copy(_tile_rows(yr_ref, src), _tile_rows(buf.at[sl, kk], r), sems.at[sl]).start()
            return c
        lax.fori_loop(0, tm, body, 0, unroll=2)

    @pl.when(step == 0)
    def _():
        gather(meta_ref, slot)

    @pl.when(step + 1 < total)
    def _():
        gather(nmeta_ref, 1 - slot)

    for kk in range(TOP_K):
        pltpu.make_async_copy(yr_ref.at[pl.ds(0, tm * SUBLANES)], buf.at[slot, kk], sems.at[slot]).wait()
    gates = gate_ref[0]
    f = gates[:, 0:1] * _load_rows_from_tiles(buf, tm, (slot, 0))
    for kk in range(1, TOP_K):
        f = f + gates[:, kk:kk + 1] * _load_rows_from_tiles(buf, tm, (slot, kk))
    o_ref[0] = x1_ref[0] + gt2_ref[0] * _rms(f, g_ref[...])


def _moe_combine(prompt, pstart, meta_tiles, tile0, yr, gates, x1, mod, g_post2):
    bsz, s, d = x1.shape
    tm = TOKEN_TILE
    per_b = s // tm
    last = tile0 + bsz * per_b - 1
    row = lambda w: pl.BlockSpec((1, tm, w), lambda b, i, pst: (b, i, 0))
    if prompt:
        gt2 = pl.BlockSpec((1, 1, d), lambda b, i, pst: (b, 0, 5))
    else:
        gt2 = pl.BlockSpec((1, tm, d), lambda b, i, pst: (0, i, 5))
    return pl.pallas_call(
        _moe_combine_kernel,
        grid_spec=pltpu.PrefetchScalarGridSpec(
            num_scalar_prefetch=1,
            grid=(bsz, per_b),
            in_specs=[_meta_spec(lambda b, i, pst: (tile0 + b * per_b + i, 0, 0)),
                      _meta_spec(lambda b, i, pst: (jnp.minimum(tile0 + b * per_b + i + 1, last), 0, 0)),
                      pl.BlockSpec(memory_space=pl.ANY),
                      row(LANES), row(d), gt2,
                      pl.BlockSpec((1, d), lambda b, i, pst: (0, 0))],
            out_specs=row(d),
            scratch_shapes=[pltpu.VMEM((2, TOP_K, tm * SUBLANES, LANES), F32), pltpu.SemaphoreType.DMA((2,))]),
        out_shape=jax.ShapeDtypeStruct((bsz, s, d), F32),
        compiler_params=_cparams(("arbitrary", "arbitrary")),
        name="moe_combine_prompt" if prompt else "moe_combine_sample",
    )(pstart, meta_tiles, meta_tiles, yr, gates, x1, mod, g_post2)


def kernel(x_prompt, x_sample, cache_k, cache_v, state_conv, page_table, c_prompt, c_sample, w_mod, b_mod,
           g_pre1, g_post1, g_pre2, g_post2, w_in, conv_w, conv_b, conv_ln_g, conv_ln_b, w_out, b_out,
           router_w, router_b, w_gate_up, b_gate_up, w_down, b_down):
    assert w_mod.shape[0] == 1, "single-layer trunk"
    bsz, s, d = x_prompt.shape
    db, t, _ = x_sample.shape
    c = conv_w.shape[2]
    keep = CONV_TAPS - 1
    rows_s = db * t

    mod = _mod(jnp.concatenate([c_prompt, c_sample], axis=0), w_mod[0], b_mod[0])
    mod_p = mod[:bsz].reshape(bsz, 1, 6 * d)
    mod_s = jnp.repeat(mod[bsz:], t, axis=0).reshape(1, rows_s, 6 * d)

    w_in_bf = w_in[0].astype(BF16)
    wo_bf = w_out[0].astype(BF16)
    cw = jnp.concatenate([conv_w[0], jnp.zeros((HALO - CONV_TAPS, c), F32)], axis=0)
    rw = jnp.concatenate([router_w[0], jnp.zeros((d, LANES - N_EXPERTS), F32)], axis=1)
    rb = jnp.concatenate([router_b[0], jnp.zeros((LANES - N_EXPERTS,), F32)]).reshape(1, LANES)
    shared = (cw, conv_b, conv_ln_g, conv_ln_b, wo_bf, b_out, g_post1, g_pre2, rw, rb)

    q_p, kb_p, vb_p, k_pages, v_pages, u_p, km_p = _in_proj_prompt(x_prompt, mod_p, g_pre1, w_in_bf)
    q_s, k_s, v_s, u_s = _in_proj_sample(x_sample, mod_s, g_pre1, w_in_bf)

    attn_p = _moba_prompt(q_p, kb_p, vb_p, km_p)
    attn_s = _moba_sample(q_s, k_s, v_s, jnp.swapaxes(cache_k[0], 2, 3), jnp.swapaxes(cache_v[0], 2, 3), page_table)
    attn_s = attn_s.transpose(0, 2, 1, 3).reshape(1, rows_s, ATTN_W).astype(BF16)

    cnt0 = jnp.zeros((1, LANES), F32)
    x1_p, h2_p, meta_p, gate_p, cnt_p = _post_mix(True, attn_p, u_p, None, x_prompt, mod_p, cnt0, *shared)
    x1_s, h2_s, meta_s, gate_s, cnt, conv_s = _post_mix(False, attn_s, u_s, state_conv[0],
                                                        x_sample.reshape(1, rows_s, d), mod_s, cnt_p, *shared)

    n_tok = bsz * s + rows_s
    counts = cnt[0, :N_EXPERTS].astype(I32)
    padded = (counts + ROW_BLOCK - 1) // ROW_BLOCK * ROW_BLOCK
    pend = jnp.cumsum(padded)
    pstart = pend - padded
    nblk = -(-(n_tok * TOP_K) // ROW_BLOCK) + N_EXPERTS
    starts = jnp.arange(nblk, dtype=I32) * ROW_BLOCK
    block_e = jnp.minimum(jnp.sum((pend[None, :] <= starts[:, None]).astype(I32), axis=1), N_EXPERTS - 1)
    n_active = (pend[-1:] // ROW_BLOCK).astype(I32)
    meta = jnp.concatenate([meta_p.reshape(bsz * s, LANES)[:, :META_W], meta_s.reshape(rows_s, LANES)[:, :META_W]],
                           axis=0)
    meta_tiles = meta.reshape(n_tok // TOKEN_TILE, 1, TOKEN_TILE * META_W)

    xr = _moe_scatter(counts, pstart, meta_tiles, h2_p, h2_s, nblk * ROW_BLOCK)
    yr = _moe_expert(block_e, n_active, xr, w_gate_up[0], b_gate_up[0], w_down[0], b_down[0])
    y_p = _moe_combine(True, pstart, meta_tiles, 0, yr, gate_p, x1_p, mod_p, g_post2)
    y_s = _moe_combine(False, pstart, meta_tiles, bsz * s // TOKEN_TILE, yr, gate_s, x1_s, mod_s, g_post2)

    conv_p = u_p[:, s - keep:, :]
    return (y_p, y_s.reshape(db, t, d), k_pages[None], v_pages[None], conv_p[None],
            k_s[None], v_s[None], conv_s[None])
```

```python
import functools

import jax
import jax.numpy as jnp
from jax import lax
from jax.experimental import pallas as pl
from jax.experimental.pallas import tpu as pltpu

F32 = jnp.float32
BF16 = jnp.bfloat16
I32 = jnp.int32

ATTN_HEADS = 8
HEAD_DIM = 64
ATTN_W = ATTN_HEADS * HEAD_DIM
CONV_TAPS = 31
MOBA_BLOCK = 256
MOBA_TOP = 3
PAGE_SIZE = 128
N_EXPERTS = 32
TOP_K = 4
SWIGLU_LIMIT = 7.0
SWIGLU_ALPHA = 1.702
RMS_EPS = 1e-6
LN_EPS = 1e-5
NEG_INF = -1e30
REMOVED = -3e38

LANES = 128
SUBLANES = 8
VMEM_LIMIT = 56 * 1024 * 1024
DMA_PRIORITIES = 2

KV_STEP = 4 * MOBA_BLOCK
SAMPLE_HEADS = 4
ROW_BLOCK = 512
TOKEN_TILE = 512
META_W = 2 * TOP_K
HALO = 32


def _cparams(sem):
    return pltpu.CompilerParams(dimension_semantics=sem, vmem_limit_bytes=VMEM_LIMIT)


def _dot(a, b):
    return jnp.dot(a, b, preferred_element_type=F32)


def _dot_nt(a, b):
    return lax.dot_general(a, b, (((1,), (1,)), ((), ())), preferred_element_type=F32)


def _split(x):
    hi = x.astype(BF16)
    lo = (x - hi.astype(F32)).astype(BF16)
    return hi, lo


def _rms(x, g):
    return x * lax.rsqrt(jnp.mean(x * x, axis=-1, keepdims=True) + RMS_EPS) * g


def _store_rows_as_tiles(ref, x):
    rows = x.shape[0]
    for c in range(SUBLANES):
        ref[pl.ds(c, rows, stride=SUBLANES), :] = x[:, c * LANES:(c + 1) * LANES]


def _load_rows_from_tiles(ref, rows, lead=()):
    return jnp.concatenate([ref[(*lead, pl.ds(c, rows, stride=SUBLANES), slice(None))] for c in range(SUBLANES)],
                           axis=1)


def _top_n(score, lane_f, n):
    out = []
    work = score
    for _ in range(n):
        m = jnp.max(work, axis=1, keepdims=True)
        idx = jnp.min(jnp.where(work == m, lane_f, 1e9), axis=1, keepdims=True)
        out.append((m, idx))
        work = jnp.where(lane_f == idx, REMOVED, work)
    return out


def _mod_kernel(c_ref, w_ref, b_ref, o_ref):
    c = c_ref[...]
    s = (c * jax.nn.sigmoid(c)).astype(BF16)
    o_ref[...] = _dot(s, w_ref[...].astype(BF16)) + b_ref[...]


def _mod(c, w, b):
    n, d = c.shape
    dout = w.shape[1]
    bn = 1536
    return pl.pallas_call(
        _mod_kernel,
        grid=(dout // bn,),
        in_specs=[pl.BlockSpec((n, d), lambda j: (0, 0)),
                  pl.BlockSpec((d, bn), lambda j: (0, j)),
                  pl.BlockSpec((1, bn), lambda j: (0, j))],
        out_specs=pl.BlockSpec((n, bn), lambda j: (0, j)),
        out_shape=jax.ShapeDtypeStruct((n, dout), F32),
        compiler_params=_cparams(("arbitrary",)),
        name="mod",
    )(c, w, b.reshape(1, dout))


def _in_proj_kernel(x_ref, sh_ref, sc_ref, g_ref, w_ref, *outs, prompt):
    x = x_ref[0]
    tm = x.shape[0]
    h = _rms(x, g_ref[...]) * (1.0 + sc_ref[0]) + sh_ref[0]
    proj = _dot(h.astype(BF16), w_ref[...])
    q = proj[:, :ATTN_W] * (HEAD_DIM ** -0.5)
    k = proj[:, ATTN_W:2 * ATTN_W]
    v = proj[:, 2 * ATTN_W:3 * ATTN_W]
    c = (proj.shape[1] - 3 * ATTN_W) // 2
    a = proj[:, 3 * ATTN_W:3 * ATTN_W + c]
    g = proj[:, 3 * ATTN_W + c:]
    u = a * jax.nn.sigmoid(g)
    if prompt:
        q_ref, kb_ref, vb_ref, kf_ref, vf_ref, u_ref, km_ref = outs
        q_ref[0] = q.astype(BF16)
        kb_ref[0] = k.astype(BF16)
        vb_ref[0] = v.astype(BF16)
        u_ref[0] = u
        for p in range(tm // PAGE_SIZE):
            rows = slice(p * PAGE_SIZE, (p + 1) * PAGE_SIZE)
            for hh in range(ATTN_HEADS):
                cols = slice(hh * HEAD_DIM, (hh + 1) * HEAD_DIM)
                kf_ref[0, p, hh] = k[rows, cols]
                vf_ref[0, p, hh] = v[rows, cols]
        for blk in range(tm // MOBA_BLOCK):
            rows = slice(blk * MOBA_BLOCK, (blk + 1) * MOBA_BLOCK)
            km_ref[0, blk] = jnp.mean(k[rows], axis=0, keepdims=True)
    else:
        q_ref, kf_ref, vf_ref, u_ref = outs
        u_ref[0] = u
        t = q_ref.shape[2]
        for hh in range(ATTN_HEADS):
            cols = slice(hh * HEAD_DIM, (hh + 1) * HEAD_DIM)
            q_ref[:, hh] = q[:, cols].reshape(tm // t, t, HEAD_DIM)
            kf_ref[:, hh] = k[:, cols].reshape(tm // t, t, HEAD_DIM)
            vf_ref[:, hh] = v[:, cols].reshape(tm // t, t, HEAD_DIM)


def _mod_spec(per_row, tm, d, chunk):
    if per_row:
        return pl.BlockSpec((1, tm, d), lambda b, i: (0, i, chunk))
    return pl.BlockSpec((1, 1, d), lambda b, i: (b, 0, chunk))


def _in_proj_prompt(x, mod, g_pre1, w_in_bf):
    bsz, s, d = x.shape
    tm = 512
    c = (w_in_bf.shape[1] - 3 * ATTN_W) // 2
    n_pages = s // PAGE_SIZE
    nb = s // MOBA_BLOCK
    row = lambda w: pl.BlockSpec((1, tm, w), lambda b, i: (b, i, 0))
    paged = pl.BlockSpec((1, tm // PAGE_SIZE, ATTN_HEADS, PAGE_SIZE, HEAD_DIM), lambda b, i: (b, i, 0, 0, 0))
    return pl.pallas_call(
        functools.partial(_in_proj_kernel, prompt=True),
        grid=(bsz, s // tm),
        in_specs=[row(d), _mod_spec(False, tm, d, 0), _mod_spec(False, tm, d, 1),
                  pl.BlockSpec((1, d), lambda b, i: (0, 0)),
                  pl.BlockSpec(w_in_bf.shape, lambda b, i: (0, 0))],
        out_specs=[row(ATTN_W), row(ATTN_W), row(ATTN_W), paged, paged, row(c),
                   pl.BlockSpec((1, tm // MOBA_BLOCK, 1, ATTN_W), lambda b, i: (b, i, 0, 0))],
        out_shape=[jax.ShapeDtypeStruct((bsz, s, ATTN_W), BF16)] * 3
        + [jax.ShapeDtypeStruct((bsz, n_pages, ATTN_HEADS, PAGE_SIZE, HEAD_DIM), F32)] * 2
        + [jax.ShapeDtypeStruct((bsz, s, c), F32),
           jax.ShapeDtypeStruct((bsz, nb, 1, ATTN_W), F32)],
        compiler_params=_cparams(("arbitrary", "arbitrary")),
        name="in_proj_prompt",
    )(x, mod, mod, g_pre1, w_in_bf)


def _in_proj_sample(x, mod_rows, g_pre1, w_in_bf):
    db, t, d = x.shape
    rows = db * t
    tm = 128
    c = (w_in_bf.shape[1] - 3 * ATTN_W) // 2
    heads = pl.BlockSpec((tm // t, ATTN_HEADS, t, HEAD_DIM), lambda b, i: (i, 0, 0, 0))
    return pl.pallas_call(
        functools.partial(_in_proj_kernel, prompt=False),
        grid=(1, rows // tm),
        in_specs=[pl.BlockSpec((1, tm, d), lambda b, i: (0, i, 0)),
                  _mod_spec(True, tm, d, 0), _mod_spec(True, tm, d, 1),
                  pl.BlockSpec((1, d), lambda b, i: (0, 0)),
                  pl.BlockSpec(w_in_bf.shape, lambda b, i: (0, 0))],
        out_specs=[heads, heads, heads, pl.BlockSpec((1, tm, c), lambda b, i: (0, i, 0))],
        out_shape=[jax.ShapeDtypeStruct((db, ATTN_HEADS, t, HEAD_DIM), F32)] * 3
        + [jax.ShapeDtypeStruct((1, rows, c), F32)],
        compiler_params=_cparams(("arbitrary", "arbitrary")),
        name="in_proj_sample",
    )(x.reshape(1, rows, d), mod_rows, mod_rows, g_pre1, w_in_bf)


def _moba_prompt_kernel(q_ref, k_ref, v_ref, km_ref, o_ref, kh_scr, vh_scr, s0_scr):
    s_len = q_ref.shape[1]
    nb = s_len // MOBA_BLOCK
    lane = lax.broadcasted_iota(I32, (1, LANES), 1)
    lane_f = lane.astype(F32)
    row_i = lax.broadcasted_iota(I32, (MOBA_BLOCK, MOBA_BLOCK), 0)
    col_i = lax.broadcasted_iota(I32, (MOBA_BLOCK, MOBA_BLOCK), 1)
    causal = col_i <= row_i
    per = KV_STEP // MOBA_BLOCK
    km = km_ref[0, :, 0, :]

    heads = []
    for hh in range(2):
        in_head = (lane >= hh * HEAD_DIM) & (lane < (hh + 1) * HEAD_DIM)
        off = HEAD_DIM * (1 - hh)
        km_h = jnp.where(in_head, km, 0.0)
        parts = []
        if off:
            parts.append(jnp.zeros((off, LANES), F32))
        parts.append(km_h)
        parts.append(jnp.zeros((LANES - off - nb, LANES), F32))
        km_hi, km_lo = _split(jnp.concatenate(parts, axis=0))
        heads.append((in_head, off, km_hi, km_lo))

    def recycle(j, carry):
        r0 = pl.multiple_of(j * MOBA_BLOCK, MOBA_BLOCK)
        kj = k_ref[0, pl.ds(r0, MOBA_BLOCK), :]
        vj = v_ref[0, pl.ds(r0, MOBA_BLOCK), :]
        for hh, (in_head, off, _, _) in enumerate(heads):
            onehot = jnp.where(lane == off + j, 1.0, 0.0).astype(BF16)
            kh_scr[hh, pl.ds(r0, MOBA_BLOCK), :] = jnp.where(in_head, kj, onehot)
            vh_scr[hh, pl.ds(r0, MOBA_BLOCK), :] = jnp.where(in_head, vj, jnp.ones_like(vj))
        return carry

    lax.fori_loop(0, nb, recycle, 0)

    def q_tile(i, carry):
        q0 = pl.multiple_of(i * MOBA_BLOCK, MOBA_BLOCK)
        q = q_ref[0, pl.ds(q0, MOBA_BLOCK), :]
        kd = k_ref[0, pl.ds(q0, MOBA_BLOCK), :]
        vd = v_ref[0, pl.ds(q0, MOBA_BLOCK), :]
        q_augs, state = [], []
        for in_head, off, km_hi, km_lo in heads:
            qh = jnp.where(in_head, q, jnp.zeros_like(q))
            blk = lane - off
            in_bias = (blk >= 0) & (blk < nb)
            valid = in_bias & (blk < i)
            score = jnp.where(valid, _dot_nt(qh, km_hi) + _dot_nt(qh, km_lo), NEG_INF)
            sel = jnp.zeros(score.shape, F32)
            for _, idx in _top_n(score, lane_f, MOBA_TOP):
                sel = jnp.where((lane_f == idx) & valid, 1.0, sel)
            bias = jnp.where(in_bias & (sel == 0.0), NEG_INF, 0.0).astype(BF16)
            q_augs.append(jnp.where(in_head, q, bias))

            s = jnp.where(causal, _dot_nt(qh, kd), NEG_INF)
            m = jnp.max(s, axis=1, keepdims=True)
            p = jnp.exp(s - m)
            state.append((m, _dot(p.astype(BF16), jnp.where(in_head, vd, jnp.ones_like(vd)))))

        def scores(hh, jj):
            k0 = pl.multiple_of(jnp.minimum(jj * KV_STEP, s_len - KV_STEP), KV_STEP)
            return _dot_nt(q_augs[hh], kh_scr[hh, pl.ds(k0, KV_STEP), :]).astype(BF16)

        def absorb(hh, jj, sb, m, acc):
            k0 = pl.multiple_of(jj * KV_STEP, KV_STEP)
            tile_max = sb[:, :LANES]
            for cc in range(1, KV_STEP // LANES):
                tile_max = jnp.maximum(tile_max, sb[:, cc * LANES:(cc + 1) * LANES])
            m_new = jnp.maximum(m, jnp.max(tile_max.astype(F32), axis=1, keepdims=True))
            alpha = jnp.exp(m - m_new)
            p = jnp.exp(sb - m_new.astype(BF16))
            return m_new, alpha * acc + _dot(p, vh_scr[hh, pl.ds(k0, KV_STEP), :])

        s0_scr[...] = scores(0, 0)

        def kv_step(jj, st):
            sb1 = scores(1, jj)
            first = absorb(0, jj, s0_scr[...], *st[0])
            second = absorb(1, jj, sb1, *st[1])
            s0_scr[...] = scores(0, jj + 1)
            return first, second

        state = lax.fori_loop(0, (i + per - 1) // per, kv_step, tuple(state))
        outs = [acc / pltpu.roll(acc, HEAD_DIM, axis=1) for _, acc in state]
        o_ref[0, pl.ds(q0, MOBA_BLOCK), :] = jnp.where(heads[0][0], outs[0], outs[1]).astype(o_ref.dtype)
        return carry

    lax.fori_loop(0, nb, q_tile, 0)


def _moba_prompt(q, k, v, km):
    bsz, s, _ = q.shape
    assert 2 * HEAD_DIM == LANES and s % KV_STEP == 0 and s // MOBA_BLOCK <= HEAD_DIM
    nb = s // MOBA_BLOCK
    spec = pl.BlockSpec((1, s, LANES), lambda b, hp: (b, 0, hp))
    return pl.pallas_call(
        _moba_prompt_kernel,
        grid=(bsz, ATTN_W // LANES),
        in_specs=[spec, spec, spec, pl.BlockSpec((1, nb, 1, LANES), lambda b, hp: (b, 0, 0, hp))],
        out_specs=spec,
        out_shape=jax.ShapeDtypeStruct((bsz, s, ATTN_W), BF16),
        scratch_shapes=[pltpu.VMEM((2, s, LANES), BF16)] * 2 + [pltpu.VMEM((MOBA_BLOCK, KV_STEP), BF16)],
        compiler_params=_cparams(("arbitrary", "arbitrary")),
        name="moba_prompt",
    )(q, k, v, km)


def _moba_sample_kernel(pt_ref, q_ref, kn_ref, vn_ref, ck_ref, cv_ref, o_ref, kbuf, vbuf, sems):
    n_pages = kbuf.shape[1]
    hg = kbuf.shape[2]
    n_groups = pl.num_programs(1)
    seq = pl.program_id(0)
    grp = pl.program_id(1)
    step = seq * n_groups + grp
    total = pl.num_programs(0) * n_groups
    slot = step % 2

    def fetch(sq, gp, sl):
        def body(p, c):
            page = pt_ref[sq * n_pages + p]
            heads = pl.ds(gp * hg, hg)
            pltpu.make_async_copy(ck_ref.at[page, heads], kbuf.at[sl, p], sems.at[0, sl]).start(priority=0)
            pltpu.make_async_copy(cv_ref.at[page, heads], vbuf.at[sl, p], sems.at[1, sl]).start(priority=1)
            return c
        lax.fori_loop(0, n_pages, body, 0, unroll=4)

    @pl.when(step == 0)
    def _():
        fetch(seq, grp, slot)

    @pl.when(step + 1 < total)
    def _():
        nxt = step + 1
        fetch(nxt // n_groups, nxt % n_groups, 1 - slot)

    pltpu.make_async_copy(ck_ref.at[pl.ds(0, n_pages), pl.ds(0, hg)], kbuf.at[slot], sems.at[0, slot]).wait()
    pltpu.make_async_copy(cv_ref.at[pl.ds(0, n_pages), pl.ds(0, hg)], vbuf.at[slot], sems.at[1, slot]).wait()

    ppb = MOBA_BLOCK // PAGE_SIZE
    nfp = n_pages // ppb
    n_sel = min(MOBA_TOP, nfp)
    t = q_ref.shape[2]
    rows = hg * t
    width = hg * HEAD_DIM
    blk_id = lax.broadcasted_iota(I32, (nfp, 1, 1), 0).astype(F32)
    page_blk = (lax.broadcasted_iota(I32, (n_pages, 1, 1), 0) // ppb).astype(F32)
    row_head = lax.broadcasted_iota(I32, (rows, 1), 0) // t
    own = row_head == lax.broadcasted_iota(I32, (1, width), 1) // HEAD_DIM
    tile = (lax.broadcasted_iota(I32, (HEAD_DIM, width), 1) % HEAD_DIM
            == lax.broadcasted_iota(I32, (HEAD_DIM, width), 0)).astype(BF16)
    r_i = lax.broadcasted_iota(I32, (rows, rows), 0)
    c_i = lax.broadcasted_iota(I32, (rows, rows), 1)
    causal = (r_i // t == c_i // t) & (c_i % t <= r_i % t)
    batch_nn = (((2,), (1,)), ((0,), (0,)))
    batch_nt = (((2,), (2,)), ((0,), (0,)))

    def block_diag(x):
        spread = _dot(x.reshape(rows, HEAD_DIM).astype(BF16), tile)
        return jnp.where(own, spread, 0.0).astype(BF16)

    q_bd = block_diag(q_ref[0])
    kn_bd = block_diag(kn_ref[0])
    vnb = vn_ref[0].reshape(rows, HEAD_DIM).astype(BF16)
    kt = kbuf[slot].reshape(n_pages, width, PAGE_SIZE).astype(BF16)
    vt = vbuf[slot].reshape(n_pages, width, PAGE_SIZE).astype(BF16)
    s = lax.dot_general(jnp.broadcast_to(q_bd[None], (n_pages, rows, width)), kt, batch_nn,
                        preferred_element_type=F32)

    score = jnp.sum(jnp.sum(s.reshape(nfp, ppb, rows, PAGE_SIZE), axis=1), axis=2, keepdims=True)
    keep = None
    for _ in range(n_sel):
        best = jnp.max(score, axis=0, keepdims=True)
        idx = jnp.min(jnp.where(score == best, blk_id, 1e9), axis=0, keepdims=True)
        score = jnp.where(blk_id == idx, REMOVED, score)
        hit = page_blk == idx
        keep = hit if keep is None else keep | hit

    s = jnp.where(keep, s, NEG_INF)
    so = jnp.where(causal, _dot_nt(q_bd, kn_bd), NEG_INF)
    m = jnp.maximum(jnp.max(jnp.max(s, axis=0), axis=1, keepdims=True), jnp.max(so, axis=1, keepdims=True))
    p = jnp.exp(s - m[None])
    po = jnp.exp(so - m)
    denom = jnp.sum(jnp.sum(p, axis=0), axis=1, keepdims=True) + jnp.sum(po, axis=1, keepdims=True)
    pv = jnp.sum(lax.dot_general(p.astype(BF16), vt, batch_nt, preferred_element_type=F32), axis=0)
    past = jnp.zeros((rows, HEAD_DIM), F32)
    for hh in range(hg):
        past = past + jnp.where(row_head == hh, pv[:, hh * HEAD_DIM:(hh + 1) * HEAD_DIM], 0.0)
    o_ref[0] = ((past + _dot(po.astype(BF16), vnb)) / denom).reshape(hg, t, HEAD_DIM)


def _moba_sample(q, kn, vn, cache_kt, cache_vt, page_table):
    db, nh, t, _ = q.shape
    n_pages = page_table.shape[1]
    hg = SAMPLE_HEADS
    assert (n_pages * PAGE_SIZE) % MOBA_BLOCK == 0 and n_pages * PAGE_SIZE >= MOBA_BLOCK and nh % hg == 0
    new = pl.BlockSpec((1, hg, t, HEAD_DIM), lambda s, g, pt: (s, g, 0, 0))
    return pl.pallas_call(
        _moba_sample_kernel,
        grid_spec=pltpu.PrefetchScalarGridSpec(
            num_scalar_prefetch=1,
            grid=(db, nh // hg),
            in_specs=[new, new, new, pl.BlockSpec(memory_space=pl.ANY), pl.BlockSpec(memory_space=pl.ANY)],
            out_specs=new,
            scratch_shapes=[pltpu.VMEM((2, n_pages, hg, HEAD_DIM, PAGE_SIZE), F32),
                            pltpu.VMEM((2, n_pages, hg, HEAD_DIM, PAGE_SIZE), F32),
                            pltpu.SemaphoreType.DMA((2, 2))]),
        out_shape=jax.ShapeDtypeStruct((db, nh, t, HEAD_DIM), F32),
        compiler_params=_cparams(("arbitrary", "arbitrary")),
        name="moba_sample",
    )(page_table.reshape(-1), q, kn, vn, cache_kt, cache_vt)


def _conv_tail(y, lng_ref, lnb_ref):
    mu = jnp.mean(y, axis=-1, keepdims=True)
    yc = y - mu
    var = jnp.mean(yc * yc, axis=-1, keepdims=True)
    z = yc * lax.rsqrt(var + LN_EPS) * lng_ref[...] + lnb_ref[...]
    return z * jax.nn.sigmoid(z)


def _post_mix_tail(attn, conv, x, gt1, sh2, sc2, wo_ref, bo_ref, gp1_ref, gp2_ref, rw_ref, rb_ref,
                   cnt_ref, x1_ref, h2_ref, meta_ref, gate_ref):
    tm = x.shape[0]
    half = attn.shape[1]
    mixed = _dot(attn, wo_ref[:half, :]) + _dot(conv.astype(BF16), wo_ref[half:, :]) + bo_ref[...]
    x1 = x + gt1 * _rms(mixed, gp1_ref[...])
    h2 = _rms(x1, gp2_ref[...]) * (1.0 + sc2) + sh2
    x1_ref[...] = x1.reshape(x1_ref.shape)
    _store_rows_as_tiles(h2_ref, h2)

    lane = lax.broadcasted_iota(I32, (1, LANES), 1)
    lane_f = lane.astype(F32)
    h_hi, h_lo = _split(h2)
    w_hi, w_lo = _split(rw_ref[...])
    logits = _dot(h_hi, w_hi) + _dot(h_hi, w_lo) + _dot(h_lo, w_hi) + rb_ref[...]
    logits = jnp.where(lane < N_EXPERTS, logits, REMOVED)
    picks = _top_n(logits, lane_f, TOP_K)
    exps = [jnp.exp(val - picks[0][0]) for val, _ in picks]
    total = exps[0]
    for e in exps[1:]:
        total = total + e
    sel = jnp.zeros((tm, LANES), F32)
    for _, idx in picks:
        sel = jnp.where(lane_f == idx, 1.0, sel)

    before = lax.broadcasted_iota(I32, (tm, tm), 1) < lax.broadcasted_iota(I32, (tm, tm), 0)
    rank_all = cnt_ref[...] + _dot(jnp.where(before, 1.0, 0.0).astype(BF16), sel.astype(BF16))
    cnt_ref[...] = cnt_ref[...] + jnp.sum(sel, axis=0, keepdims=True)

    meta = jnp.zeros((tm, LANES), F32)
    gates = jnp.zeros((tm, LANES), F32)
    for kk, (_, idx) in enumerate(picks):
        rank = jnp.sum(jnp.where(lane_f == idx, rank_all, 0.0), axis=1, keepdims=True)
        meta = jnp.where(lane == kk, idx, meta)
        meta = jnp.where(lane == TOP_K + kk, rank, meta)
        gates = jnp.where(lane == kk, exps[kk] / total, gates)
    meta_ref[...] = meta.astype(I32).reshape(meta_ref.shape)
    gate_ref[...] = gates.reshape(gate_ref.shape)


def _post_mix_prompt_kernel(attn_ref, u_ref, halo_ref, x_ref, gt1_ref, sh2_ref, sc2_ref, cw_ref, cb_ref,
                            lng_ref, lnb_ref, wo_ref, bo_ref, gp1_ref, gp2_ref, rw_ref, rb_ref, cnt0_ref,
                            x1_ref, h2_ref, meta_ref, gate_ref, cnt_ref, ext):
    b = pl.program_id(0)
    i = pl.program_id(1)
    tm = u_ref.shape[1]

    @pl.when((b == 0) & (i == 0))
    def _():
        cnt_ref[...] = cnt0_ref[...]

    ext[0:HALO, :] = jnp.where(i > 0, halo_ref[0], 0.0)
    ext[HALO:HALO + tm, :] = u_ref[0]
    first = HALO - (CONV_TAPS - 1)
    y = cb_ref[...] + cw_ref[0:1, :] * ext[first:first + tm, :]
    for j in range(1, CONV_TAPS):
        y = y + cw_ref[j:j + 1, :] * ext[first + j:first + j + tm, :]
    conv = _conv_tail(y, lng_ref, lnb_ref)
    _post_mix_tail(attn_ref[0], conv, x_ref[0], gt1_ref[0], sh2_ref[0], sc2_ref[0], wo_ref, bo_ref,
                   gp1_ref, gp2_ref, rw_ref, rb_ref, cnt_ref, x1_ref, h2_ref, meta_ref, gate_ref)


def _post_mix_sample_kernel(attn_ref, u_ref, st_ref, x_ref, gt1_ref, sh2_ref, sc2_ref, cw_ref, cb_ref,
                            lng_ref, lnb_ref, wo_ref, bo_ref, gp1_ref, gp2_ref, rw_ref, rb_ref, cnt0_ref,
                            x1_ref, h2_ref, meta_ref, gate_ref, cnt_ref, st_out_ref, ext):
    i = pl.program_id(1)
    nseq, keep, c = st_ref.shape
    t = u_ref.shape[1] // nseq

    @pl.when(i == 0)
    def _():
        cnt_ref[...] = cnt0_ref[...]

    ext[:, 0:keep, :] = st_ref[...]
    ext[:, keep:keep + t, :] = u_ref[0].reshape(nseq, t, c)
    y = cb_ref[...] + cw_ref[0:1, :] * ext[:, 0:t, :]
    for j in range(1, CONV_TAPS):
        y = y + cw_ref[j:j + 1, :] * ext[:, j:j + t, :]
    st_out_ref[...] = ext[:, t:t + keep, :]
    conv = _conv_tail(y.reshape(nseq * t, c), lng_ref, lnb_ref)
    _post_mix_tail(attn_ref[0], conv, x_ref[0], gt1_ref[0], sh2_ref[0], sc2_ref[0], wo_ref, bo_ref,
                   gp1_ref, gp2_ref, rw_ref, rb_ref, cnt_ref, x1_ref, h2_ref, meta_ref, gate_ref)


def _post_mix(prompt, attn, u, prefix, x, mod, cnt0, cw, cb, lng, lnb, wo_bf, bo, gp1, gp2, rw, rb):
    bsz, s, d = x.shape
    c = u.shape[2]
    tm = 256 if prompt else 128
    row = lambda w: pl.BlockSpec((1, tm, w), lambda b, i: (b, i, 0))
    full = lambda a: pl.BlockSpec(a.shape, lambda b, i: (0,) * a.ndim)
    weights = (cw, cb, lng, lnb, wo_bf, bo, gp1, gp2, rw, rb, cnt0)
    mods = [_mod_spec(not prompt, tm, d, ch) for ch in (2, 3, 4)]
    assert d == SUBLANES * LANES, "a token row of the MoE input must fill exactly one (8,128) tile"
    per_b = s // tm
    tiles = pl.BlockSpec((tm * SUBLANES, LANES), lambda b, i: (b * per_b + i, 0))
    out_specs = [row(d), tiles, row(LANES), row(LANES), pl.BlockSpec((1, LANES), lambda b, i: (0, 0))]
    out_shape = [jax.ShapeDtypeStruct((bsz, s, d), F32), jax.ShapeDtypeStruct((bsz * s * SUBLANES, LANES), F32),
                 jax.ShapeDtypeStruct((bsz, s, LANES), I32), jax.ShapeDtypeStruct((bsz, s, LANES), F32),
                 jax.ShapeDtypeStruct((1, LANES), F32)]
    if prompt:
        kern = _post_mix_prompt_kernel
        per = tm // HALO
        side = pl.BlockSpec((1, HALO, c), lambda b, i: (b, jnp.maximum(i * per - 1, 0), 0))
        side_arg = u
        scratch = [pltpu.VMEM((HALO + tm, c), F32)]
    else:
        kern = _post_mix_sample_kernel
        nseq, keep = prefix.shape[0], prefix.shape[1]
        t = s // nseq
        per = tm // t
        side = pl.BlockSpec((per, keep, c), lambda b, i: (i, 0, 0))
        side_arg = prefix
        out_specs.append(pl.BlockSpec((per, keep, c), lambda b, i: (i, 0, 0)))
        out_shape.append(jax.ShapeDtypeStruct(prefix.shape, F32))
        scratch = [pltpu.VMEM((per, -(-(keep + t) // SUBLANES) * SUBLANES, c), F32)]
    return pl.pallas_call(
        kern,
        grid=(bsz, s // tm),
        in_specs=[row(ATTN_W), row(c), side, row(d)] + mods + [full(a) for a in weights],
        out_specs=out_specs,
        out_shape=out_shape,
        scratch_shapes=scratch,
        compiler_params=_cparams(("arbitrary", "arbitrary")),
        name="post_mix_prompt" if prompt else "post_mix_sample",
    )(attn, u, side_arg, x, mod, mod, mod, *weights)


def _meta_spec(index):
    return pl.BlockSpec((1, 1, TOKEN_TILE * META_W), index, memory_space=pltpu.SMEM)


def _tile_rows(ref, row):
    return ref.at[pl.ds(pl.multiple_of(row * SUBLANES, SUBLANES), SUBLANES)]


def _moe_scatter_kernel(cnt_ref, pst_ref, meta_ref, hp_ref, hs_ref, xr_ref, sem, *, tiles_p, n_pad):
    i = pl.program_id(0)
    n_rows = xr_ref.shape[0] // SUBLANES

    def wait_tiles(n):
        for _ in range(n):
            pltpu.make_async_copy(hp_ref, xr_ref.at[pl.ds(0, TOKEN_TILE * SUBLANES)], sem).wait()

    @pl.when(i == 0)
    def _():
        def fill(lo, hi):
            def body(r, c):
                pltpu.make_async_copy(_tile_rows(hp_ref, 0), _tile_rows(xr_ref, r), sem).start()
                return c
            lax.fori_loop(lo, hi, body, 0)

        end = 0
        for e in range(N_EXPERTS):
            used = pst_ref[e] + cnt_ref[e]
            end = pst_ref[e] + (cnt_ref[e] + ROW_BLOCK - 1) // ROW_BLOCK * ROW_BLOCK
            fill(used, end)
        fill(end, n_rows)
        wait_tiles(n_pad // TOKEN_TILE)

    def scatter_tile(src_ref):
        def body(r, c):
            for kk in range(TOP_K):
                dst = pst_ref[meta_ref[0, 0, r * META_W + kk]] + meta_ref[0, 0, r * META_W + TOP_K + kk]
                pltpu.make_async_copy(_tile_rows(src_ref, r), _tile_rows(xr_ref, dst), sem).start(
                    priority=kk % DMA_PRIORITIES)
            return c
        lax.fori_loop(0, TOKEN_TILE, body, 0, unroll=2)
        wait_tiles(TOP_K)

    @pl.when(i < tiles_p)
    def _():
        scatter_tile(hp_ref)

    @pl.when(i >= tiles_p)
    def _():
        scatter_tile(hs_ref)


def _moe_scatter(counts, pstart, meta_tiles, h_p, h_s, n_rows):
    rows_t = TOKEN_TILE * SUBLANES
    tiles_p = h_p.shape[0] // rows_t
    n_tok = (h_p.shape[0] + h_s.shape[0]) // SUBLANES
    n_pad = n_rows - n_tok * TOP_K
    assert h_p.shape[0] % rows_t == 0 and h_s.shape[0] % rows_t == 0 and n_pad % TOKEN_TILE == 0
    return pl.pallas_call(
        functools.partial(_moe_scatter_kernel, tiles_p=tiles_p, n_pad=n_pad),
        grid_spec=pltpu.PrefetchScalarGridSpec(
            num_scalar_prefetch=2,
            grid=(meta_tiles.shape[0],),
            in_specs=[_meta_spec(lambda i, cnt, pst: (i, 0, 0)),
                      pl.BlockSpec((rows_t, LANES), lambda i, cnt, pst: (jnp.minimum(i, tiles_p - 1), 0)),
                      pl.BlockSpec((rows_t, LANES), lambda i, cnt, pst: (jnp.maximum(i - tiles_p, 0), 0))],
            out_specs=pl.BlockSpec(memory_space=pl.ANY),
            scratch_shapes=[pltpu.SemaphoreType.DMA(())]),
        out_shape=jax.ShapeDtypeStruct((n_rows * SUBLANES, LANES), F32),
        compiler_params=_cparams(("arbitrary",)),
        name="moe_scatter",
    )(counts, pstart, meta_tiles, h_p, h_s)


def _moe_expert_kernel(be_ref, na_ref, x_ref, wgu_ref, bgu_ref, wd_ref, bd_ref, y_ref, wgu_bf, wd_bf):
    b = pl.program_id(0)
    active = b < na_ref[0]
    prev = be_ref[jnp.maximum(b - 1, 0)]

    @pl.when(active & ((b == 0) | (be_ref[b] != prev)))
    def _():
        wgu_bf[...] = wgu_ref[0].astype(BF16)
        wd_bf[...] = wd_ref[0].astype(BF16)

    @pl.when(active)
    def _():
        de = wd_bf.shape[0]
        x = _load_rows_from_tiles(x_ref, ROW_BLOCK)
        gu = _dot(x.astype(BF16), wgu_bf[...]) + bgu_ref[0]
        glu = jnp.minimum(gu[:, :de], SWIGLU_LIMIT)
        lin = jnp.clip(gu[:, de:], -SWIGLU_LIMIT, SWIGLU_LIMIT)
        act = glu * jax.nn.sigmoid(SWIGLU_ALPHA * glu) * (lin + 1.0)
        _store_rows_as_tiles(y_ref, _dot(act.astype(BF16), wd_bf[...]) + bd_ref[0])

    @pl.when(jnp.logical_not(active))
    def _():
        y_ref[...] = jnp.zeros(y_ref.shape, y_ref.dtype)


def _moe_expert(block_e, n_active, xr, wgu, bgu, wd, bd):
    ne, d, two_de = wgu.shape
    de = wd.shape[1]
    nblk = xr.shape[0] // (ROW_BLOCK * SUBLANES)
    tiles = lambda index: pl.BlockSpec((ROW_BLOCK * SUBLANES, LANES), index)
    last = lambda b, na: jnp.minimum(b, na[0] - 1)
    return pl.pallas_call(
        _moe_expert_kernel,
        grid_spec=pltpu.PrefetchScalarGridSpec(
            num_scalar_prefetch=2,
            grid=(nblk,),
            in_specs=[tiles(lambda b, be, na: (last(b, na), 0)),
                      pl.BlockSpec((1, d, two_de), lambda b, be, na: (be[last(b, na)], 0, 0)),
                      pl.BlockSpec((1, 1, two_de), lambda b, be, na: (be[last(b, na)], 0, 0)),
                      pl.BlockSpec((1, de, d), lambda b, be, na: (be[last(b, na)], 0, 0)),
                      pl.BlockSpec((1, 1, d), lambda b, be, na: (be[last(b, na)], 0, 0))],
            out_specs=tiles(lambda b, be, na: (b, 0)),
            scratch_shapes=[pltpu.VMEM((d, two_de), BF16), pltpu.VMEM((de, d), BF16)]),
        out_shape=jax.ShapeDtypeStruct(xr.shape, F32),
        compiler_params=_cparams(("arbitrary",)),
        name="moe_expert",
    )(block_e, n_active, xr, wgu, bgu.reshape(ne, 1, two_de), wd, bd.reshape(ne, 1, d))


def _moe_combine_kernel(pst_ref, meta_ref, nmeta_ref, yr_ref, gate_ref, x1_ref, gt2_ref, g_ref, o_ref, buf, sems):
    step = pl.program_id(0) * pl.num_programs(1) + pl.program_id(1)
    total = pl.num_programs(0) * pl.num_programs(1)
    slot = step % 2
    tm = buf.shape[2] // SUBLANES

    def gather(m_ref, sl):
        def body(r, c):
            for kk in range(TOP_K):
                src = pst_ref[m_ref[0, 0, r * META_W + kk]] + m_ref[0, 0, r * META_W + TOP_K + kk]
                pltpu.make_async_copy(_tile_rows(yr_ref, src), _tile_rows(buf.at[sl, kk], r), sems.at[sl]).start(
                    priority=kk % DMA_PRIORITIES)
            return c
        lax.fori_loop(0, tm, body, 0, unroll=2)

    @pl.when(step == 0)
    def _():
        gather(meta_ref, slot)

    @pl.when(step + 1 < total)
    def _():
        gather(nmeta_ref, 1 - slot)

    for kk in range(TOP_K):
        pltpu.make_async_copy(yr_ref.at[pl.ds(0, tm * SUBLANES)], buf.at[slot, kk], sems.at[slot]).wait()
    gates = gate_ref[0]
    f = gates[:, 0:1] * _load_rows_from_tiles(buf, tm, (slot, 0))
    for kk in range(1, TOP_K):
        f = f + gates[:, kk:kk + 1] * _load_rows_from_tiles(buf, tm, (slot, kk))
    o_ref[0] = x1_ref[0] + gt2_ref[0] * _rms(f, g_ref[...])


def _moe_combine(prompt, pstart, meta_tiles, tile0, yr, gates, x1, mod, g_post2):
    bsz, s, d = x1.shape
    tm = TOKEN_TILE
    per_b = s // tm
    last = tile0 + bsz * per_b - 1
    row = lambda w: pl.BlockSpec((1, tm, w), lambda b, i, pst: (b, i, 0))
    if prompt:
        gt2 = pl.BlockSpec((1, 1, d), lambda b, i, pst: (b, 0, 5))
    else:
        gt2 = pl.BlockSpec((1, tm, d), lambda b, i, pst: (0, i, 5))
    return pl.pallas_call(
        _moe_combine_kernel,
        grid_spec=pltpu.PrefetchScalarGridSpec(
            num_scalar_prefetch=1,
            grid=(bsz, per_b),
            in_specs=[_meta_spec(lambda b, i, pst: (tile0 + b * per_b + i, 0, 0)),
                      _meta_spec(lambda b, i, pst: (jnp.minimum(tile0 + b * per_b + i + 1, last), 0, 0)),
                      pl.BlockSpec(memory_space=pl.ANY),
                      row(LANES), row(d), gt2,
                      pl.BlockSpec((1, d), lambda b, i, pst: (0, 0))],
            out_specs=row(d),
            scratch_shapes=[pltpu.VMEM((2, TOP_K, tm * SUBLANES, LANES), F32), pltpu.SemaphoreType.DMA((2,))]),
        out_shape=jax.ShapeDtypeStruct((bsz, s, d), F32),
        compiler_params=_cparams(("arbitrary", "arbitrary")),
        name="moe_combine_prompt" if prompt else "moe_combine_sample",
    )(pstart, meta_tiles, meta_tiles, yr, gates, x1, mod, g_post2)


def kernel(x_prompt, x_sample, cache_k, cache_v, state_conv, page_table, c_prompt, c_sample, w_mod, b_mod,
           g_pre1, g_post1, g_pre2, g_post2, w_in, conv_w, conv_b, conv_ln_g, conv_ln_b, w_out, b_out,
           router_w, router_b, w_gate_up, b_gate_up, w_down, b_down):
    assert w_mod.shape[0] == 1, "single-layer trunk"
    bsz, s, d = x_prompt.shape
    db, t, _ = x_sample.shape
    c = conv_w.shape[2]
    keep = CONV_TAPS - 1
    rows_s = db * t

    mod = _mod(jnp.concatenate([c_prompt, c_sample], axis=0), w_mod[0], b_mod[0])
    mod_p = mod[:bsz].reshape(bsz, 1, 6 * d)
    mod_s = jnp.repeat(mod[bsz:], t, axis=0).reshape(1, rows_s, 6 * d)

    w_in_bf = w_in[0].astype(BF16)
    wo_bf = w_out[0].astype(BF16)
    cw = jnp.concatenate([conv_w[0], jnp.zeros((HALO - CONV_TAPS, c), F32)], axis=0)
    rw = jnp.concatenate([router_w[0], jnp.zeros((d, LANES - N_EXPERTS), F32)], axis=1)
    rb = jnp.concatenate([router_b[0], jnp.zeros((LANES - N_EXPERTS,), F32)]).reshape(1, LANES)
    shared = (cw, conv_b, conv_ln_g, conv_ln_b, wo_bf, b_out, g_post1, g_pre2, rw, rb)

    q_p, kb_p, vb_p, k_pages, v_pages, u_p, km_p = _in_proj_prompt(x_prompt, mod_p, g_pre1, w_in_bf)
    q_s, k_s, v_s, u_s = _in_proj_sample(x_sample, mod_s, g_pre1, w_in_bf)

    attn_p = _moba_prompt(q_p, kb_p, vb_p, km_p)
    attn_s = _moba_sample(q_s, k_s, v_s, jnp.swapaxes(cache_k[0], 2, 3), jnp.swapaxes(cache_v[0], 2, 3), page_table)
    attn_s = attn_s.transpose(0, 2, 1, 3).reshape(1, rows_s, ATTN_W).astype(BF16)

    cnt0 = jnp.zeros((1, LANES), F32)
    x1_p, h2_p, meta_p, gate_p, cnt_p = _post_mix(True, attn_p, u_p, None, x_prompt, mod_p, cnt0, *shared)
    x1_s, h2_s, meta_s, gate_s, cnt, conv_s = _post_mix(False, attn_s, u_s, state_conv[0],
                                                        x_sample.reshape(1, rows_s, d), mod_s, cnt_p, *shared)

    n_tok = bsz * s + rows_s
    counts = cnt[0, :N_EXPERTS].astype(I32)
    padded = (counts + ROW_BLOCK - 1) // ROW_BLOCK * ROW_BLOCK
    pend = jnp.cumsum(padded)
    pstart = pend - padded
    nblk = -(-(n_tok * TOP_K) // ROW_BLOCK) + N_EXPERTS
    starts = jnp.arange(nblk, dtype=I32) * ROW_BLOCK
    block_e = jnp.minimum(jnp.sum((pend[None, :] <= starts[:, None]).astype(I32), axis=1), N_EXPERTS - 1)
    n_active = (pend[-1:] // ROW_BLOCK).astype(I32)
    meta = jnp.concatenate([meta_p.reshape(bsz * s, LANES)[:, :META_W], meta_s.reshape(rows_s, LANES)[:, :META_W]],
                           axis=0)
    meta_tiles = meta.reshape(n_tok // TOKEN_TILE, 1, TOKEN_TILE * META_W)

    xr = _moe_scatter(counts, pstart, meta_tiles, h2_p, h2_s, nblk * ROW_BLOCK)
    yr = _moe_expert(block_e, n_active, xr, w_gate_up[0], b_gate_up[0], w_down[0], b_down[0])
    y_p = _moe_combine(True, pstart, meta_tiles, 0, yr, gate_p, x1_p, mod_p, g_post2)
    y_s = _moe_combine(False, pstart, meta_tiles, bsz * s // TOKEN_TILE, yr, gate_s, x1_s, mod_s, g_post2)

    conv_p = u_p[:, s - keep:, :]
    return (y_p, y_s.reshape(db, t, d), k_pages[None], v_pages[None], conv_p[None],
            k_s[None], v_s[None], conv_s[None])
```

```python
import functools

import jax
import jax.numpy as jnp
from jax import lax
from jax.experimental import pallas as pl
from jax.experimental.pallas import tpu as pltpu

F32 = jnp.float32
BF16 = jnp.bfloat16
I32 = jnp.int32

ATTN_HEADS = 8
HEAD_DIM = 64
ATTN_W = ATTN_HEADS * HEAD_DIM
CONV_TAPS = 31
MOBA_BLOCK = 256
MOBA_TOP = 3
PAGE_SIZE = 128
N_EXPERTS = 32
TOP_K = 4
SWIGLU_LIMIT = 7.0
SWIGLU_ALPHA = 1.702
RMS_EPS = 1e-6
LN_EPS = 1e-5
NEG_INF = -1e30
REMOVED = -3e38

LANES = 128
SUBLANES = 8
VMEM_LIMIT = 56 * 1024 * 1024
DMA_PRIORITIES = 2

KV_STEP = 4 * MOBA_BLOCK
SAMPLE_HEADS = 4
ROW_BLOCK = 512
TOKEN_TILE = 512
HALO = 32


def _cparams(sem):
    return pltpu.CompilerParams(dimension_semantics=sem, vmem_limit_bytes=VMEM_LIMIT)


def _dot(a, b):
    return jnp.dot(a, b, preferred_element_type=F32)


def _dot_nt(a, b):
    return lax.dot_general(a, b, (((1,), (1,)), ((), ())), preferred_element_type=F32)


def _split(x):
    hi = x.astype(BF16)
    lo = (x - hi.astype(F32)).astype(BF16)
    return hi, lo


def _rms(x, g):
    return x * lax.rsqrt(jnp.mean(x * x, axis=-1, keepdims=True) + RMS_EPS) * g


def _store_rows_as_tiles(ref, x):
    rows = x.shape[0]
    for c in range(SUBLANES):
        ref[pl.ds(c, rows, stride=SUBLANES), :] = x[:, c * LANES:(c + 1) * LANES]


def _load_rows_from_tiles(ref, rows, lead=()):
    return jnp.concatenate([ref[(*lead, pl.ds(c, rows, stride=SUBLANES), slice(None))] for c in range(SUBLANES)],
                           axis=1)


def _top_n(score, lane_f, n):
    out = []
    work = score
    for _ in range(n):
        m = jnp.max(work, axis=1, keepdims=True)
        idx = jnp.min(jnp.where(work == m, lane_f, 1e9), axis=1, keepdims=True)
        out.append((m, idx))
        work = jnp.where(lane_f == idx, REMOVED, work)
    return out


def _mod_kernel(c_ref, w_ref, b_ref, o_ref):
    c = c_ref[...]
    s = (c * jax.nn.sigmoid(c)).astype(BF16)
    o_ref[...] = _dot(s, w_ref[...].astype(BF16)) + b_ref[...]


def _mod(c, w, b):
    n, d = c.shape
    dout = w.shape[1]
    bn = 1536
    return pl.pallas_call(
        _mod_kernel,
        grid=(dout // bn,),
        in_specs=[pl.BlockSpec((n, d), lambda j: (0, 0)),
                  pl.BlockSpec((d, bn), lambda j: (0, j)),
                  pl.BlockSpec((1, bn), lambda j: (0, j))],
        out_specs=pl.BlockSpec((n, bn), lambda j: (0, j)),
        out_shape=jax.ShapeDtypeStruct((n, dout), F32),
        compiler_params=_cparams(("arbitrary",)),
        name="mod",
    )(c, w, b.reshape(1, dout))


def _in_proj_kernel(x_ref, sh_ref, sc_ref, g_ref, w_ref, *outs, prompt):
    x = x_ref[0]
    tm = x.shape[0]
    h = _rms(x, g_ref[...]) * (1.0 + sc_ref[0]) + sh_ref[0]
    proj = _dot(h.astype(BF16), w_ref[...])
    q = proj[:, :ATTN_W] * (HEAD_DIM ** -0.5)
    k = proj[:, ATTN_W:2 * ATTN_W]
    v = proj[:, 2 * ATTN_W:3 * ATTN_W]
    c = (proj.shape[1] - 3 * ATTN_W) // 2
    a = proj[:, 3 * ATTN_W:3 * ATTN_W + c]
    g = proj[:, 3 * ATTN_W + c:]
    u = a * jax.nn.sigmoid(g)
    if prompt:
        q_ref, kb_ref, vb_ref, kf_ref, vf_ref, u_ref, km_ref = outs
        q_ref[0] = q.astype(BF16)
        kb_ref[0] = k.astype(BF16)
        vb_ref[0] = v.astype(BF16)
        u_ref[0] = u
        for p in range(tm // PAGE_SIZE):
            rows = slice(p * PAGE_SIZE, (p + 1) * PAGE_SIZE)
            for hh in range(ATTN_HEADS):
                cols = slice(hh * HEAD_DIM, (hh + 1) * HEAD_DIM)
                kf_ref[0, p, hh] = k[rows, cols]
                vf_ref[0, p, hh] = v[rows, cols]
        for blk in range(tm // MOBA_BLOCK):
            rows = slice(blk * MOBA_BLOCK, (blk + 1) * MOBA_BLOCK)
            km_ref[0, blk] = jnp.mean(k[rows], axis=0, keepdims=True)
    else:
        q_ref, kf_ref, vf_ref, u_ref = outs
        u_ref[0] = u
        t = q_ref.shape[2]
        for hh in range(ATTN_HEADS):
            cols = slice(hh * HEAD_DIM, (hh + 1) * HEAD_DIM)
            q_ref[:, hh] = q[:, cols].reshape(tm // t, t, HEAD_DIM)
            kf_ref[:, hh] = k[:, cols].reshape(tm // t, t, HEAD_DIM)
            vf_ref[:, hh] = v[:, cols].reshape(tm // t, t, HEAD_DIM)


def _mod_spec(per_row, tm, d, chunk):
    if per_row:
        return pl.BlockSpec((1, tm, d), lambda b, i: (0, i, chunk))
    return pl.BlockSpec((1, 1, d), lambda b, i: (b, 0, chunk))


def _in_proj_prompt(x, mod, g_pre1, w_in_bf):
    bsz, s, d = x.shape
    tm = 512
    c = (w_in_bf.shape[1] - 3 * ATTN_W) // 2
    n_pages = s // PAGE_SIZE
    nb = s // MOBA_BLOCK
    row = lambda w: pl.BlockSpec((1, tm, w), lambda b, i: (b, i, 0))
    paged = pl.BlockSpec((1, tm // PAGE_SIZE, ATTN_HEADS, PAGE_SIZE, HEAD_DIM), lambda b, i: (b, i, 0, 0, 0))
    return pl.pallas_call(
        functools.partial(_in_proj_kernel, prompt=True),
        grid=(bsz, s // tm),
        in_specs=[row(d), _mod_spec(False, tm, d, 0), _mod_spec(False, tm, d, 1),
                  pl.BlockSpec((1, d), lambda b, i: (0, 0)),
                  pl.BlockSpec(w_in_bf.shape, lambda b, i: (0, 0))],
        out_specs=[row(ATTN_W), row(ATTN_W), row(ATTN_W), paged, paged, row(c),
                   pl.BlockSpec((1, tm // MOBA_BLOCK, 1, ATTN_W), lambda b, i: (b, i, 0, 0))],
        out_shape=[jax.ShapeDtypeStruct((bsz, s, ATTN_W), BF16)] * 3
        + [jax.ShapeDtypeStruct((bsz, n_pages, ATTN_HEADS, PAGE_SIZE, HEAD_DIM), F32)] * 2
        + [jax.ShapeDtypeStruct((bsz, s, c), F32),
           jax.ShapeDtypeStruct((bsz, nb, 1, ATTN_W), F32)],
        compiler_params=_cparams(("arbitrary", "arbitrary")),
        name="in_proj_prompt",
    )(x, mod, mod, g_pre1, w_in_bf)


def _in_proj_sample(x, mod_rows, g_pre1, w_in_bf):
    db, t, d = x.shape
    rows = db * t
    tm = 128
    c = (w_in_bf.shape[1] - 3 * ATTN_W) // 2
    heads = pl.BlockSpec((tm // t, ATTN_HEADS, t, HEAD_DIM), lambda b, i: (i, 0, 0, 0))
    return pl.pallas_call(
        functools.partial(_in_proj_kernel, prompt=False),
        grid=(1, rows // tm),
        in_specs=[pl.BlockSpec((1, tm, d), lambda b, i: (0, i, 0)),
                  _mod_spec(True, tm, d, 0), _mod_spec(True, tm, d, 1),
                  pl.BlockSpec((1, d), lambda b, i: (0, 0)),
                  pl.BlockSpec(w_in_bf.shape, lambda b, i: (0, 0))],
        out_specs=[heads, heads, heads, pl.BlockSpec((1, tm, c), lambda b, i: (0, i, 0))],
        out_shape=[jax.ShapeDtypeStruct((db, ATTN_HEADS, t, HEAD_DIM), F32)] * 3
        + [jax.ShapeDtypeStruct((1, rows, c), F32)],
        compiler_params=_cparams(("arbitrary", "arbitrary")),
        name="in_proj_sample",
    )(x.reshape(1, rows, d), mod_rows, mod_rows, g_pre1, w_in_bf)


def _moba_prompt_kernel(q_ref, k_ref, v_ref, km_ref, o_ref, kh_scr, vh_scr, s0_scr):
    s_len = q_ref.shape[1]
    nb = s_len // MOBA_BLOCK
    lane = lax.broadcasted_iota(I32, (1, LANES), 1)
    lane_f = lane.astype(F32)
    row_i = lax.broadcasted_iota(I32, (MOBA_BLOCK, MOBA_BLOCK), 0)
    col_i = lax.broadcasted_iota(I32, (MOBA_BLOCK, MOBA_BLOCK), 1)
    causal = col_i <= row_i
    per = KV_STEP // MOBA_BLOCK
    km = km_ref[0, :, 0, :]

    heads = []
    for hh in range(2):
        in_head = (lane >= hh * HEAD_DIM) & (lane < (hh + 1) * HEAD_DIM)
        off = HEAD_DIM * (1 - hh)
        km_h = jnp.where(in_head, km, 0.0)
        parts = []
        if off:
            parts.append(jnp.zeros((off, LANES), F32))
        parts.append(km_h)
        parts.append(jnp.zeros((LANES - off - nb, LANES), F32))
        km_hi, km_lo = _split(jnp.concatenate(parts, axis=0))
        heads.append((in_head, off, km_hi, km_lo))

    def recycle(j, carry):
        r0 = pl.multiple_of(j * MOBA_BLOCK, MOBA_BLOCK)
        kj = k_ref[0, pl.ds(r0, MOBA_BLOCK), :]
        vj = v_ref[0, pl.ds(r0, MOBA_BLOCK), :]
        for hh, (in_head, off, _, _) in enumerate(heads):
            onehot = jnp.where(lane == off + j, 1.0, 0.0).astype(BF16)
            kh_scr[hh, pl.ds(r0, MOBA_BLOCK), :] = jnp.where(in_head, kj, onehot)
            vh_scr[hh, pl.ds(r0, MOBA_BLOCK), :] = jnp.where(in_head, vj, jnp.ones_like(vj))
        return carry

    lax.fori_loop(0, nb, recycle, 0)

    def q_tile(i, carry):
        q0 = pl.multiple_of(i * MOBA_BLOCK, MOBA_BLOCK)
        q = q_ref[0, pl.ds(q0, MOBA_BLOCK), :]
        kd = k_ref[0, pl.ds(q0, MOBA_BLOCK), :]
        vd = v_ref[0, pl.ds(q0, MOBA_BLOCK), :]
        q_augs, state = [], []
        for in_head, off, km_hi, km_lo in heads:
            qh = jnp.where(in_head, q, jnp.zeros_like(q))
            blk = lane - off
            in_bias = (blk >= 0) & (blk < nb)
            valid = in_bias & (blk < i)
            score = jnp.where(valid, _dot_nt(qh, km_hi) + _dot_nt(qh, km_lo), NEG_INF)
            sel = jnp.zeros(score.shape, F32)
            for _, idx in _top_n(score, lane_f, MOBA_TOP):
                sel = jnp.where((lane_f == idx) & valid, 1.0, sel)
            bias = jnp.where(in_bias & (sel == 0.0), NEG_INF, 0.0).astype(BF16)
            q_augs.append(jnp.where(in_head, q, bias))

            s = jnp.where(causal, _dot_nt(qh, kd), NEG_INF)
            m = jnp.max(s, axis=1, keepdims=True)
            p = jnp.exp(s - m)
            state.append((m, _dot(p.astype(BF16), jnp.where(in_head, vd, jnp.ones_like(vd)))))

        def scores(hh, jj):
            k0 = pl.multiple_of(jnp.minimum(jj * KV_STEP, s_len - KV_STEP), KV_STEP)
            return _dot_nt(q_augs[hh], kh_scr[hh, pl.ds(k0, KV_STEP), :]).astype(BF16)

        def absorb(hh, jj, sb, m, acc):
            k0 = pl.multiple_of(jj * KV_STEP, KV_STEP)
            tile_max = sb[:, :LANES]
            for cc in range(1, KV_STEP // LANES):
                tile_max = jnp.maximum(tile_max, sb[:, cc * LANES:(cc + 1) * LANES])
            m_new = jnp.maximum(m, jnp.max(tile_max.astype(F32), axis=1, keepdims=True))
            alpha = jnp.exp(m - m_new)
            p = jnp.exp(sb - m_new.astype(BF16))
            return m_new, alpha * acc + _dot(p, vh_scr[hh, pl.ds(k0, KV_STEP), :])

        s0_scr[...] = scores(0, 0)

        def kv_step(jj, st):
            sb1 = scores(1, jj)
            first = absorb(0, jj, s0_scr[...], *st[0])
            second = absorb(1, jj, sb1, *st[1])
            s0_scr[...] = scores(0, jj + 1)
            return first, second

        state = lax.fori_loop(0, (i + per - 1) // per, kv_step, tuple(state))
        outs = [acc / pltpu.roll(acc, HEAD_DIM, axis=1) for _, acc in state]
        o_ref[0, pl.ds(q0, MOBA_BLOCK), :] = jnp.where(heads[0][0], outs[0], outs[1]).astype(o_ref.dtype)
        return carry

    lax.fori_loop(0, nb, q_tile, 0)


def _moba_prompt(q, k, v, km):
    bsz, s, _ = q.shape
    assert 2 * HEAD_DIM == LANES and s % KV_STEP == 0 and s // MOBA_BLOCK <= HEAD_DIM
    nb = s // MOBA_BLOCK
    spec = pl.BlockSpec((1, s, LANES), lambda b, hp: (b, 0, hp))
    return pl.pallas_call(
        _moba_prompt_kernel,
        grid=(bsz, ATTN_W // LANES),
        in_specs=[spec, spec, spec, pl.BlockSpec((1, nb, 1, LANES), lambda b, hp: (b, 0, 0, hp))],
        out_specs=spec,
        out_shape=jax.ShapeDtypeStruct((bsz, s, ATTN_W), BF16),
        scratch_shapes=[pltpu.VMEM((2, s, LANES), BF16)] * 2 + [pltpu.VMEM((MOBA_BLOCK, KV_STEP), BF16)],
        compiler_params=_cparams(("arbitrary", "arbitrary")),
        name="moba_prompt",
    )(q, k, v, km)


def _moba_sample_kernel(pt_ref, q_ref, kn_ref, vn_ref, ck_ref, cv_ref, o_ref, kbuf, vbuf, sems):
    n_pages = kbuf.shape[1]
    hg = kbuf.shape[2]
    n_groups = pl.num_programs(1)
    seq = pl.program_id(0)
    grp = pl.program_id(1)
    step = seq * n_groups + grp
    total = pl.num_programs(0) * n_groups
    slot = step % 2

    def fetch(sq, gp, sl):
        def body(p, c):
            page = pt_ref[sq * n_pages + p]
            heads = pl.ds(gp * hg, hg)
            pltpu.make_async_copy(ck_ref.at[page, heads], kbuf.at[sl, p], sems.at[0, sl]).start()
            pltpu.make_async_copy(cv_ref.at[page, heads], vbuf.at[sl, p], sems.at[1, sl]).start()
            return c
        lax.fori_loop(0, n_pages, body, 0, unroll=4)

    @pl.when(step == 0)
    def _():
        fetch(seq, grp, slot)

    @pl.when(step + 1 < total)
    def _():
        nxt = step + 1
        fetch(nxt // n_groups, nxt % n_groups, 1 - slot)

    pltpu.make_async_copy(ck_ref.at[pl.ds(0, n_pages), pl.ds(0, hg)], kbuf.at[slot], sems.at[0, slot]).wait()
    pltpu.make_async_copy(cv_ref.at[pl.ds(0, n_pages), pl.ds(0, hg)], vbuf.at[slot], sems.at[1, slot]).wait()

    ppb = MOBA_BLOCK // PAGE_SIZE
    nfp = n_pages // ppb
    n_sel = min(MOBA_TOP, nfp)
    t = q_ref.shape[2]
    rows = hg * t
    width = hg * HEAD_DIM
    blk_id = lax.broadcasted_iota(I32, (nfp, 1, 1), 0).astype(F32)
    page_blk = (lax.broadcasted_iota(I32, (n_pages, 1, 1), 0) // ppb).astype(F32)
    row_head = lax.broadcasted_iota(I32, (rows, 1), 0) // t
    own = row_head == lax.broadcasted_iota(I32, (1, width), 1) // HEAD_DIM
    tile = (lax.broadcasted_iota(I32, (HEAD_DIM, width), 1) % HEAD_DIM
            == lax.broadcasted_iota(I32, (HEAD_DIM, width), 0)).astype(BF16)
    r_i = lax.broadcasted_iota(I32, (rows, rows), 0)
    c_i = lax.broadcasted_iota(I32, (rows, rows), 1)
    causal = (r_i // t == c_i // t) & (c_i % t <= r_i % t)
    batch_nn = (((2,), (1,)), ((0,), (0,)))
    batch_nt = (((2,), (2,)), ((0,), (0,)))

    def block_diag(x):
        spread = _dot(x.reshape(rows, HEAD_DIM).astype(BF16), tile)
        return jnp.where(own, spread, 0.0).astype(BF16)

    q_bd = block_diag(q_ref[0])
    kn_bd = block_diag(kn_ref[0])
    vnb = vn_ref[0].reshape(rows, HEAD_DIM).astype(BF16)
    kt = kbuf[slot].reshape(n_pages, width, PAGE_SIZE).astype(BF16)
    vt = vbuf[slot].reshape(n_pages, width, PAGE_SIZE).astype(BF16)
    s = lax.dot_general(jnp.broadcast_to(q_bd[None], (n_pages, rows, width)), kt, batch_nn,
                        preferred_element_type=F32)

    score = jnp.sum(jnp.sum(s.reshape(nfp, ppb, rows, PAGE_SIZE), axis=1), axis=2, keepdims=True)
    keep = None
    for _ in range(n_sel):
        best = jnp.max(score, axis=0, keepdims=True)
        idx = jnp.min(jnp.where(score == best, blk_id, 1e9), axis=0, keepdims=True)
        score = jnp.where(blk_id == idx, REMOVED, score)
        hit = page_blk == idx
        keep = hit if keep is None else keep | hit

    s = jnp.where(keep, s, NEG_INF)
    so = jnp.where(causal, _dot_nt(q_bd, kn_bd), NEG_INF)
    m = jnp.maximum(jnp.max(jnp.max(s, axis=0), axis=1, keepdims=True), jnp.max(so, axis=1, keepdims=True))
    p = jnp.exp(s - m[None])
    po = jnp.exp(so - m)
    denom = jnp.sum(jnp.sum(p, axis=0), axis=1, keepdims=True) + jnp.sum(po, axis=1, keepdims=True)
    pv = jnp.sum(lax.dot_general(p.astype(BF16), vt, batch_nt, preferred_element_type=F32), axis=0)
    past = jnp.zeros((rows, HEAD_DIM), F32)
    for hh in range(hg):
        past = past + jnp.where(row_head == hh, pv[:, hh * HEAD_DIM:(hh + 1) * HEAD_DIM], 0.0)
    o_ref[0] = ((past + _dot(po.astype(BF16), vnb)) / denom).reshape(hg, t, HEAD_DIM)


def _moba_sample(q, kn, vn, cache_kt, cache_vt, page_table):
    db, nh, t, _ = q.shape
    n_pages = page_table.shape[1]
    hg = SAMPLE_HEADS
    assert (n_pages * PAGE_SIZE) % MOBA_BLOCK == 0 and n_pages * PAGE_SIZE >= MOBA_BLOCK and nh % hg == 0
    new = pl.BlockSpec((1, hg, t, HEAD_DIM), lambda s, g, pt: (s, g, 0, 0))
    return pl.pallas_call(
        _moba_sample_kernel,
        grid_spec=pltpu.PrefetchScalarGridSpec(
            num_scalar_prefetch=1,
            grid=(db, nh // hg),
            in_specs=[new, new, new, pl.BlockSpec(memory_space=pl.ANY), pl.BlockSpec(memory_space=pl.ANY)],
            out_specs=new,
            scratch_shapes=[pltpu.VMEM((2, n_pages, hg, HEAD_DIM, PAGE_SIZE), F32),
                            pltpu.VMEM((2, n_pages, hg, HEAD_DIM, PAGE_SIZE), F32),
                            pltpu.SemaphoreType.DMA((2, 2))]),
        out_shape=jax.ShapeDtypeStruct((db, nh, t, HEAD_DIM), F32),
        compiler_params=_cparams(("arbitrary", "arbitrary")),
        name="moba_sample",
    )(page_table.reshape(-1), q, kn, vn, cache_kt, cache_vt)


def _conv_tail(y, lng_ref, lnb_ref):
    mu = jnp.mean(y, axis=-1, keepdims=True)
    yc = y - mu
    var = jnp.mean(yc * yc, axis=-1, keepdims=True)
    z = yc * lax.rsqrt(var + LN_EPS) * lng_ref[...] + lnb_ref[...]
    return z * jax.nn.sigmoid(z)


def _post_mix_tail(attn, conv, x, gt1, sh2, sc2, wo_ref, bo_ref, gp1_ref, gp2_ref, rw_ref, rb_ref,
                   cnt_ref, x1_ref, h2_ref, meta_ref, gate_ref):
    tm = x.shape[0]
    half = attn.shape[1]
    mixed = _dot(attn, wo_ref[:half, :]) + _dot(conv.astype(BF16), wo_ref[half:, :]) + bo_ref[...]
    x1 = x + gt1 * _rms(mixed, gp1_ref[...])
    h2 = _rms(x1, gp2_ref[...]) * (1.0 + sc2) + sh2
    x1_ref[...] = x1.reshape(x1_ref.shape)
    _store_rows_as_tiles(h2_ref, h2)

    lane = lax.broadcasted_iota(I32, (1, LANES), 1)
    lane_f = lane.astype(F32)
    h_hi, h_lo = _split(h2)
    w_hi, w_lo = _split(rw_ref[...])
    logits = _dot(h_hi, w_hi) + _dot(h_hi, w_lo) + _dot(h_lo, w_hi) + rb_ref[...]
    logits = jnp.where(lane < N_EXPERTS, logits, REMOVED)
    picks = _top_n(logits, lane_f, TOP_K)
    exps = [jnp.exp(val - picks[0][0]) for val, _ in picks]
    total = exps[0]
    for e in exps[1:]:
        total = total + e
    sel = jnp.zeros((tm, LANES), F32)
    for _, idx in picks:
        sel = jnp.where(lane_f == idx, 1.0, sel)

    before = lax.broadcasted_iota(I32, (tm, tm), 1) < lax.broadcasted_iota(I32, (tm, tm), 0)
    rank_all = cnt_ref[...] + _dot(jnp.where(before, 1.0, 0.0).astype(BF16), sel.astype(BF16))
    cnt_ref[...] = cnt_ref[...] + jnp.sum(sel, axis=0, keepdims=True)

    meta = jnp.zeros((tm, LANES), F32)
    gates = jnp.zeros((tm, LANES), F32)
    for kk, (_, idx) in enumerate(picks):
        rank = jnp.sum(jnp.where(lane_f == idx, rank_all, 0.0), axis=1, keepdims=True)
        meta = jnp.where(lane == kk, idx, meta)
        meta = jnp.where(lane == TOP_K + kk, rank, meta)
        gates = jnp.where(lane == kk, exps[kk] / total, gates)
    meta_ref[...] = meta.astype(I32).reshape(meta_ref.shape)
    gate_ref[...] = gates.reshape(gate_ref.shape)


def _post_mix_prompt_kernel(attn_ref, u_ref, halo_ref, x_ref, gt1_ref, sh2_ref, sc2_ref, cw_ref, cb_ref,
                            lng_ref, lnb_ref, wo_ref, bo_ref, gp1_ref, gp2_ref, rw_ref, rb_ref, cnt0_ref,
                            x1_ref, h2_ref, meta_ref, gate_ref, cnt_ref, ext):
    b = pl.program_id(0)
    i = pl.program_id(1)
    tm = u_ref.shape[1]

    @pl.when((b == 0) & (i == 0))
    def _():
        cnt_ref[...] = cnt0_ref[...]

    ext[0:HALO, :] = jnp.where(i > 0, halo_ref[0], 0.0)
    ext[HALO:HALO + tm, :] = u_ref[0]
    first = HALO - (CONV_TAPS - 1)
    y = cb_ref[...] + cw_ref[0:1, :] * ext[first:first + tm, :]
    for j in range(1, CONV_TAPS):
        y = y + cw_ref[j:j + 1, :] * ext[first + j:first + j + tm, :]
    conv = _conv_tail(y, lng_ref, lnb_ref)
    _post_mix_tail(attn_ref[0], conv, x_ref[0], gt1_ref[0], sh2_ref[0], sc2_ref[0], wo_ref, bo_ref,
                   gp1_ref, gp2_ref, rw_ref, rb_ref, cnt_ref, x1_ref, h2_ref, meta_ref, gate_ref)


def _post_mix_sample_kernel(attn_ref, u_ref, st_ref, x_ref, gt1_ref, sh2_ref, sc2_ref, cw_ref, cb_ref,
                            lng_ref, lnb_ref, wo_ref, bo_ref, gp1_ref, gp2_ref, rw_ref, rb_ref, cnt0_ref,
                            x1_ref, h2_ref, meta_ref, gate_ref, cnt_ref, st_out_ref, ext):
    i = pl.program_id(1)
    nseq, keep, c = st_ref.shape
    t = u_ref.shape[1] // nseq

    @pl.when(i == 0)
    def _():
        cnt_ref[...] = cnt0_ref[...]

    ext[:, 0:keep, :] = st_ref[...]
    ext[:, keep:keep + t, :] = u_ref[0].reshape(nseq, t, c)
    y = cb_ref[...] + cw_ref[0:1, :] * ext[:, 0:t, :]
    for j in range(1, CONV_TAPS):
        y = y + cw_ref[j:j + 1, :] * ext[:, j:j + t, :]
    st_out_ref[...] = ext[:, t:t + keep, :]
    conv = _conv_tail(y.reshape(nseq * t, c), lng_ref, lnb_ref)
    _post_mix_tail(attn_ref[0], conv, x_ref[0], gt1_ref[0], sh2_ref[0], sc2_ref[0], wo_ref, bo_ref,
                   gp1_ref, gp2_ref, rw_ref, rb_ref, cnt_ref, x1_ref, h2_ref, meta_ref, gate_ref)


def _post_mix(prompt, attn, u, prefix, x, mod, cnt0, cw, cb, lng, lnb, wo_bf, bo, gp1, gp2, rw, rb):
    bsz, s, d = x.shape
    c = u.shape[2]
    tm = 256 if prompt else 128
    row = lambda w: pl.BlockSpec((1, tm, w), lambda b, i: (b, i, 0))
    full = lambda a: pl.BlockSpec(a.shape, lambda b, i: (0,) * a.ndim)
    weights = (cw, cb, lng, lnb, wo_bf, bo, gp1, gp2, rw, rb, cnt0)
    mods = [_mod_spec(not prompt, tm, d, ch) for ch in (2, 3, 4)]
    assert d == SUBLANES * LANES, "a token row of the MoE input must fill exactly one (8,128) tile"
    per_b = s // tm
    tiles = pl.BlockSpec((tm * SUBLANES, LANES), lambda b, i: (b * per_b + i, 0))
    out_specs = [row(d), tiles, row(LANES), row(LANES), pl.BlockSpec((1, LANES), lambda b, i: (0, 0))]
    out_shape = [jax.ShapeDtypeStruct((bsz, s, d), F32), jax.ShapeDtypeStruct((bsz * s * SUBLANES, LANES), F32),
                 jax.ShapeDtypeStruct((bsz, s, LANES), I32), jax.ShapeDtypeStruct((bsz, s, LANES), F32),
                 jax.ShapeDtypeStruct((1, LANES), F32)]
    if prompt:
        kern = _post_mix_prompt_kernel
        per = tm // HALO
        side = pl.BlockSpec((1, HALO, c), lambda b, i: (b, jnp.maximum(i * per - 1, 0), 0))
        side_arg = u
        scratch = [pltpu.VMEM((HALO + tm, c), F32)]
    else:
        kern = _post_mix_sample_kernel
        nseq, keep = prefix.shape[0], prefix.shape[1]
        t = s // nseq
        per = tm // t
        side = pl.BlockSpec((per, keep, c), lambda b, i: (i, 0, 0))
        side_arg = prefix
        out_specs.append(pl.BlockSpec((per, keep, c), lambda b, i: (i, 0, 0)))
        out_shape.append(jax.ShapeDtypeStruct(prefix.shape, F32))
        scratch = [pltpu.VMEM((per, -(-(keep + t) // SUBLANES) * SUBLANES, c), F32)]
    return pl.pallas_call(
        kern,
        grid=(bsz, s // tm),
        in_specs=[row(ATTN_W), row(c), side, row(d)] + mods + [full(a) for a in weights],
        out_specs=out_specs,
        out_shape=out_shape,
        scratch_shapes=scratch,
        compiler_params=_cparams(("arbitrary", "arbitrary")),
        name="post_mix_prompt" if prompt else "post_mix_sample",
    )(attn, u, side_arg, x, mod, mod, mod, *weights)


def _pos_spec(index):
    return pl.BlockSpec((1, 1, TOKEN_TILE * TOP_K), index, memory_space=pltpu.SMEM)


def _tile_rows(ref, row):
    return ref.at[pl.ds(pl.multiple_of(row * SUBLANES, SUBLANES), SUBLANES)]


def _moe_scatter_kernel(cnt_ref, pst_ref, pos_ref, hp_ref, hs_ref, xr_ref, sem, *, tiles_p, n_pad):
    i = pl.program_id(0)
    n_rows = xr_ref.shape[0] // SUBLANES

    def wait_tiles(n):
        for _ in range(n):
            pltpu.make_async_copy(hp_ref, xr_ref.at[pl.ds(0, TOKEN_TILE * SUBLANES)], sem).wait()

    @pl.when(i == 0)
    def _():
        def fill(lo, hi):
            def body(r, c):
                pltpu.make_async_copy(_tile_rows(hp_ref, 0), _tile_rows(xr_ref, r), sem).start()
                return c
            lax.fori_loop(lo, hi, body, 0)

        end = 0
        for e in range(N_EXPERTS):
            used = pst_ref[e] + cnt_ref[e]
            end = pst_ref[e] + (cnt_ref[e] + ROW_BLOCK - 1) // ROW_BLOCK * ROW_BLOCK
            fill(used, end)
        fill(end, n_rows)
        wait_tiles(n_pad // TOKEN_TILE)

    def scatter_tile(src_ref):
        def body(r, c):
            for kk in range(TOP_K):
                dst = pos_ref[0, 0, r * TOP_K + kk]
                pltpu.make_async_copy(_tile_rows(src_ref, r), _tile_rows(xr_ref, dst), sem).start(
                    priority=kk % DMA_PRIORITIES)
            return c
        lax.fori_loop(0, TOKEN_TILE, body, 0, unroll=4)
        wait_tiles(TOP_K)

    @pl.when(i < tiles_p)
    def _():
        scatter_tile(hp_ref)

    @pl.when(i >= tiles_p)
    def _():
        scatter_tile(hs_ref)


def _moe_scatter(counts, pstart, pos_tiles, h_p, h_s, n_rows):
    rows_t = TOKEN_TILE * SUBLANES
    tiles_p = h_p.shape[0] // rows_t
    n_tok = (h_p.shape[0] + h_s.shape[0]) // SUBLANES
    n_pad = n_rows - n_tok * TOP_K
    assert h_p.shape[0] % rows_t == 0 and h_s.shape[0] % rows_t == 0 and n_pad % TOKEN_TILE == 0
    return pl.pallas_call(
        functools.partial(_moe_scatter_kernel, tiles_p=tiles_p, n_pad=n_pad),
        grid_spec=pltpu.PrefetchScalarGridSpec(
            num_scalar_prefetch=2,
            grid=(pos_tiles.shape[0],),
            in_specs=[_pos_spec(lambda i, cnt, pst: (i, 0, 0)),
                      pl.BlockSpec((rows_t, LANES), lambda i, cnt, pst: (jnp.minimum(i, tiles_p - 1), 0)),
                      pl.BlockSpec((rows_t, LANES), lambda i, cnt, pst: (jnp.maximum(i - tiles_p, 0), 0))],
            out_specs=pl.BlockSpec(memory_space=pl.ANY),
            scratch_shapes=[pltpu.SemaphoreType.DMA(())]),
        out_shape=jax.ShapeDtypeStruct((n_rows * SUBLANES, LANES), F32),
        compiler_params=_cparams(("arbitrary",)),
        name="moe_scatter",
    )(counts, pstart, pos_tiles, h_p, h_s)


def _moe_expert_kernel(be_ref, na_ref, x_ref, wgu_ref, bgu_ref, wd_ref, bd_ref, y_ref, wgu_bf, wd_bf):
    b = pl.program_id(0)
    active = b < na_ref[0]
    prev = be_ref[jnp.maximum(b - 1, 0)]

    @pl.when(active & ((b == 0) | (be_ref[b] != prev)))
    def _():
        wgu_bf[...] = wgu_ref[0].astype(BF16)
        wd_bf[...] = wd_ref[0].astype(BF16)

    @pl.when(active)
    def _():
        de = wd_bf.shape[0]
        x = _load_rows_from_tiles(x_ref, ROW_BLOCK)
        gu = _dot(x.astype(BF16), wgu_bf[...]) + bgu_ref[0]
        glu = jnp.minimum(gu[:, :de], SWIGLU_LIMIT)
        lin = jnp.clip(gu[:, de:], -SWIGLU_LIMIT, SWIGLU_LIMIT)
        act = glu * jax.nn.sigmoid(SWIGLU_ALPHA * glu) * (lin + 1.0)
        _store_rows_as_tiles(y_ref, _dot(act.astype(BF16), wd_bf[...]) + bd_ref[0])

    @pl.when(jnp.logical_not(active))
    def _():
        y_ref[...] = jnp.zeros(y_ref.shape, y_ref.dtype)


def _moe_expert(block_e, n_active, xr, wgu, bgu, wd, bd):
    ne, d, two_de = wgu.shape
    de = wd.shape[1]
    nblk = xr.shape[0] // (ROW_BLOCK * SUBLANES)
    tiles = lambda index: pl.BlockSpec((ROW_BLOCK * SUBLANES, LANES), index)
    last = lambda b, na: jnp.minimum(b, na[0] - 1)
    return pl.pallas_call(
        _moe_expert_kernel,
        grid_spec=pltpu.PrefetchScalarGridSpec(
            num_scalar_prefetch=2,
            grid=(nblk,),
            in_specs=[tiles(lambda b, be, na: (last(b, na), 0)),
                      pl.BlockSpec((1, d, two_de), lambda b, be, na: (be[last(b, na)], 0, 0)),
                      pl.BlockSpec((1, 1, two_de), lambda b, be, na: (be[last(b, na)], 0, 0)),
                      pl.BlockSpec((1, de, d), lambda b, be, na: (be[last(b, na)], 0, 0)),
                      pl.BlockSpec((1, 1, d), lambda b, be, na: (be[last(b, na)], 0, 0))],
            out_specs=tiles(lambda b, be, na: (b, 0)),
            scratch_shapes=[pltpu.VMEM((d, two_de), BF16), pltpu.VMEM((de, d), BF16)]),
        out_shape=jax.ShapeDtypeStruct(xr.shape, F32),
        compiler_params=_cparams(("arbitrary",)),
        name="moe_expert",
    )(block_e, n_active, xr, wgu, bgu.reshape(ne, 1, two_de), wd, bd.reshape(ne, 1, d))


def _moe_combine_kernel(pos_ref, npos_ref, yr_ref, gate_ref, x1_ref, gt2_ref, g_ref, o_ref, buf, sems):
    step = pl.program_id(0) * pl.num_programs(1) + pl.program_id(1)
    total = pl.num_programs(0) * pl.num_programs(1)
    slot = step % 2
    tm = buf.shape[2] // SUBLANES

    def gather(m_ref, sl):
        def body(r, c):
            for kk in range(TOP_K):
                src = m_ref[0, 0, r * TOP_K + kk]
                pltpu.make_async_copy(_tile_rows(yr_ref, src), _tile_rows(buf.at[sl, kk], r), sems.at[sl]).start(
                    priority=kk % DMA_PRIORITIES)
            return c
        lax.fori_loop(0, tm, body, 0, unroll=4)

    @pl.when(step == 0)
    def _():
        gather(pos_ref, slot)

    @pl.when(step + 1 < total)
    def _():
        gather(npos_ref, 1 - slot)

    for kk in range(TOP_K):
        pltpu.make_async_copy(yr_ref.at[pl.ds(0, tm * SUBLANES)], buf.at[slot, kk], sems.at[slot]).wait()
    gates = gate_ref[0]
    f = gates[:, 0:1] * _load_rows_from_tiles(buf, tm, (slot, 0))
    for kk in range(1, TOP_K):
        f = f + gates[:, kk:kk + 1] * _load_rows_from_tiles(buf, tm, (slot, kk))
    o_ref[0] = x1_ref[0] + gt2_ref[0] * _rms(f, g_ref[...])


def _moe_combine(prompt, pos_tiles, tile0, yr, gates, x1, mod, g_post2):
    bsz, s, d = x1.shape
    tm = TOKEN_TILE
    per_b = s // tm
    last = tile0 + bsz * per_b - 1
    row = lambda w: pl.BlockSpec((1, tm, w), lambda b, i: (b, i, 0))
    if prompt:
        gt2 = pl.BlockSpec((1, 1, d), lambda b, i: (b, 0, 5))
    else:
        gt2 = pl.BlockSpec((1, tm, d), lambda b, i: (0, i, 5))
    return pl.pallas_call(
        _moe_combine_kernel,
        grid=(bsz, per_b),
        in_specs=[_pos_spec(lambda b, i: (tile0 + b * per_b + i, 0, 0)),
                  _pos_spec(lambda b, i: (jnp.minimum(tile0 + b * per_b + i + 1, last), 0, 0)),
                  pl.BlockSpec(memory_space=pl.ANY),
                  row(LANES), row(d), gt2,
                  pl.BlockSpec((1, d), lambda b, i: (0, 0))],
        out_specs=row(d),
        scratch_shapes=[pltpu.VMEM((2, TOP_K, tm * SUBLANES, LANES), F32), pltpu.SemaphoreType.DMA((2,))],
        out_shape=jax.ShapeDtypeStruct((bsz, s, d), F32),
        compiler_params=_cparams(("arbitrary", "arbitrary")),
        name="moe_combine_prompt" if prompt else "moe_combine_sample",
    )(pos_tiles, pos_tiles, yr, gates, x1, mod, g_post2)


def kernel(x_prompt, x_sample, cache_k, cache_v, state_conv, page_table, c_prompt, c_sample, w_mod, b_mod,
           g_pre1, g_post1, g_pre2, g_post2, w_in, conv_w, conv_b, conv_ln_g, conv_ln_b, w_out, b_out,
           router_w, router_b, w_gate_up, b_gate_up, w_down, b_down):
    assert w_mod.shape[0] == 1, "single-layer trunk"
    bsz, s, d = x_prompt.shape
    db, t, _ = x_sample.shape
    c = conv_w.shape[2]
    keep = CONV_TAPS - 1
    rows_s = db * t

    mod = _mod(jnp.concatenate([c_prompt, c_sample], axis=0), w_mod[0], b_mod[0])
    mod_p = mod[:bsz].reshape(bsz, 1, 6 * d)
    mod_s = jnp.repeat(mod[bsz:], t, axis=0).reshape(1, rows_s, 6 * d)

    w_in_bf = w_in[0].astype(BF16)
    wo_bf = w_out[0].astype(BF16)
    cw = jnp.concatenate([conv_w[0], jnp.zeros((HALO - CONV_TAPS, c), F32)], axis=0)
    rw = jnp.concatenate([router_w[0], jnp.zeros((d, LANES - N_EXPERTS), F32)], axis=1)
    rb = jnp.concatenate([router_b[0], jnp.zeros((LANES - N_EXPERTS,), F32)]).reshape(1, LANES)
    shared = (cw, conv_b, conv_ln_g, conv_ln_b, wo_bf, b_out, g_post1, g_pre2, rw, rb)

    q_p, kb_p, vb_p, k_pages, v_pages, u_p, km_p = _in_proj_prompt(x_prompt, mod_p, g_pre1, w_in_bf)
    q_s, k_s, v_s, u_s = _in_proj_sample(x_sample, mod_s, g_pre1, w_in_bf)

    attn_p = _moba_prompt(q_p, kb_p, vb_p, km_p)
    attn_s = _moba_sample(q_s, k_s, v_s, jnp.swapaxes(cache_k[0], 2, 3), jnp.swapaxes(cache_v[0], 2, 3), page_table)
    attn_s = attn_s.transpose(0, 2, 1, 3).reshape(1, rows_s, ATTN_W).astype(BF16)

    cnt0 = jnp.zeros((1, LANES), F32)
    x1_p, h2_p, meta_p, gate_p, cnt_p = _post_mix(True, attn_p, u_p, None, x_prompt, mod_p, cnt0, *shared)
    x1_s, h2_s, meta_s, gate_s, cnt, conv_s = _post_mix(False, attn_s, u_s, state_conv[0],
                                                        x_sample.reshape(1, rows_s, d), mod_s, cnt_p, *shared)

    n_tok = bsz * s + rows_s
    counts = cnt[0, :N_EXPERTS].astype(I32)
    padded = (counts + ROW_BLOCK - 1) // ROW_BLOCK * ROW_BLOCK
    pend = jnp.cumsum(padded)
    pstart = pend - padded
    nblk = -(-(n_tok * TOP_K) // ROW_BLOCK) + N_EXPERTS
    starts = jnp.arange(nblk, dtype=I32) * ROW_BLOCK
    block_e = jnp.minimum(jnp.sum((pend[None, :] <= starts[:, None]).astype(I32), axis=1), N_EXPERTS - 1)
    n_active = (pend[-1:] // ROW_BLOCK).astype(I32)
    meta = jnp.concatenate([meta_p.reshape(bsz * s, LANES)[:, :2 * TOP_K],
                            meta_s.reshape(rows_s, LANES)[:, :2 * TOP_K]], axis=0)
    pos = pstart[meta[:, :TOP_K]] + meta[:, TOP_K:]
    pos_tiles = pos.reshape(n_tok // TOKEN_TILE, 1, TOKEN_TILE * TOP_K)

    xr = _moe_scatter(counts, pstart, pos_tiles, h2_p, h2_s, nblk * ROW_BLOCK)
    yr = _moe_expert(block_e, n_active, xr, w_gate_up[0], b_gate_up[0], w_down[0], b_down[0])
    y_p = _moe_combine(True, pos_tiles, 0, yr, gate_p, x1_p, mod_p, g_post2)
    y_s = _moe_combine(False, pos_tiles, bsz * s // TOKEN_TILE, yr, gate_s, x1_s, mod_s, g_post2)

    conv_p = u_p[:, s - keep:, :]
    return (y_p, y_s.reshape(db, t, d), k_pages[None], v_pages[None], conv_p[None],
            k_s[None], v_s[None], conv_s[None])
```

```python
import functools

import jax
import jax.numpy as jnp
from jax import lax
from jax.experimental import pallas as pl
from jax.experimental.pallas import tpu as pltpu

F32 = jnp.float32
BF16 = jnp.bfloat16
I32 = jnp.int32

ATTN_HEADS = 8
HEAD_DIM = 64
ATTN_W = ATTN_HEADS * HEAD_DIM
CONV_TAPS = 31
MOBA_BLOCK = 256
MOBA_TOP = 3
PAGE_SIZE = 128
N_EXPERTS = 32
TOP_K = 4
SWIGLU_LIMIT = 7.0
SWIGLU_ALPHA = 1.702
RMS_EPS = 1e-6
LN_EPS = 1e-5
NEG_INF = -1e30
REMOVED = -3e38

LANES = 128
SUBLANES = 8
VMEM_LIMIT = 56 * 1024 * 1024
DMA_PRIORITIES = 2

KV_STEP = 4 * MOBA_BLOCK
SAMPLE_HEADS = 4
ROW_BLOCK = 512
TOKEN_TILE = 512
HALO = 32


def _cparams(sem):
    return pltpu.CompilerParams(dimension_semantics=sem, vmem_limit_bytes=VMEM_LIMIT)


def _dot(a, b):
    return jnp.dot(a, b, preferred_element_type=F32)


def _dot_nt(a, b):
    return lax.dot_general(a, b, (((1,), (1,)), ((), ())), preferred_element_type=F32)


def _split(x):
    hi = x.astype(BF16)
    lo = (x - hi.astype(F32)).astype(BF16)
    return hi, lo


def _rms(x, g):
    return x * lax.rsqrt(jnp.mean(x * x, axis=-1, keepdims=True) + RMS_EPS) * g


def _store_rows_as_tiles(ref, x):
    rows = x.shape[0]
    for c in range(SUBLANES):
        ref[pl.ds(c, rows, stride=SUBLANES), :] = x[:, c * LANES:(c + 1) * LANES]


def _load_rows_from_tiles(ref, rows, lead=()):
    return jnp.concatenate([ref[(*lead, pl.ds(c, rows, stride=SUBLANES), slice(None))] for c in range(SUBLANES)],
                           axis=1)


def _top_n(score, lane_f, n):
    out = []
    work = score
    for _ in range(n):
        m = jnp.max(work, axis=1, keepdims=True)
        idx = jnp.min(jnp.where(work == m, lane_f, 1e9), axis=1, keepdims=True)
        out.append((m, idx))
        work = jnp.where(lane_f == idx, REMOVED, work)
    return out


def _mod_kernel(c_ref, w_ref, b_ref, o_ref):
    c = c_ref[...]
    s = (c * jax.nn.sigmoid(c)).astype(BF16)
    o_ref[...] = _dot(s, w_ref[...].astype(BF16)) + b_ref[...]


def _mod(c, w, b):
    n, d = c.shape
    dout = w.shape[1]
    bn = 1536
    return pl.pallas_call(
        _mod_kernel,
        grid=(dout // bn,),
        in_specs=[pl.BlockSpec((n, d), lambda j: (0, 0)),
                  pl.BlockSpec((d, bn), lambda j: (0, j)),
                  pl.BlockSpec((1, bn), lambda j: (0, j))],
        out_specs=pl.BlockSpec((n, bn), lambda j: (0, j)),
        out_shape=jax.ShapeDtypeStruct((n, dout), F32),
        compiler_params=_cparams(("arbitrary",)),
        name="mod",
    )(c, w, b.reshape(1, dout))


def _in_proj_kernel(x_ref, sh_ref, sc_ref, g_ref, w_ref, *outs, prompt):
    x = x_ref[0]
    tm = x.shape[0]
    h = _rms(x, g_ref[...]) * (1.0 + sc_ref[0]) + sh_ref[0]
    proj = _dot(h.astype(BF16), w_ref[...])
    q = proj[:, :ATTN_W] * (HEAD_DIM ** -0.5)
    k = proj[:, ATTN_W:2 * ATTN_W]
    v = proj[:, 2 * ATTN_W:3 * ATTN_W]
    c = (proj.shape[1] - 3 * ATTN_W) // 2
    a = proj[:, 3 * ATTN_W:3 * ATTN_W + c]
    g = proj[:, 3 * ATTN_W + c:]
    u = a * jax.nn.sigmoid(g)
    if prompt:
        q_ref, kb_ref, vb_ref, kf_ref, vf_ref, u_ref, km_ref = outs
        q_ref[0] = q.astype(BF16)
        kb_ref[0] = k.astype(BF16)
        vb_ref[0] = v.astype(BF16)
        u_ref[0] = u
        for p in range(tm // PAGE_SIZE):
            rows = slice(p * PAGE_SIZE, (p + 1) * PAGE_SIZE)
            for hh in range(ATTN_HEADS):
                cols = slice(hh * HEAD_DIM, (hh + 1) * HEAD_DIM)
                kf_ref[0, p, hh] = k[rows, cols]
                vf_ref[0, p, hh] = v[rows, cols]
        for blk in range(tm // MOBA_BLOCK):
            rows = slice(blk * MOBA_BLOCK, (blk + 1) * MOBA_BLOCK)
            km_ref[0, blk] = jnp.mean(k[rows], axis=0, keepdims=True)
    else:
        q_ref, kf_ref, vf_ref, u_ref = outs
        u_ref[0] = u
        t = q_ref.shape[2]
        for hh in range(ATTN_HEADS):
            cols = slice(hh * HEAD_DIM, (hh + 1) * HEAD_DIM)
            q_ref[:, hh] = q[:, cols].reshape(tm // t, t, HEAD_DIM)
            kf_ref[:, hh] = k[:, cols].reshape(tm // t, t, HEAD_DIM)
            vf_ref[:, hh] = v[:, cols].reshape(tm // t, t, HEAD_DIM)


def _mod_spec(per_row, tm, d, chunk):
    if per_row:
        return pl.BlockSpec((1, tm, d), lambda b, i: (0, i, chunk))
    return pl.BlockSpec((1, 1, d), lambda b, i: (b, 0, chunk))


def _in_proj_prompt(x, mod, g_pre1, w_in_bf):
    bsz, s, d = x.shape
    tm = 512
    c = (w_in_bf.shape[1] - 3 * ATTN_W) // 2
    n_pages = s // PAGE_SIZE
    nb = s // MOBA_BLOCK
    row = lambda w: pl.BlockSpec((1, tm, w), lambda b, i: (b, i, 0))
    paged = pl.BlockSpec((1, tm // PAGE_SIZE, ATTN_HEADS, PAGE_SIZE, HEAD_DIM), lambda b, i: (b, i, 0, 0, 0))
    return pl.pallas_call(
        functools.partial(_in_proj_kernel, prompt=True),
        grid=(bsz, s // tm),
        in_specs=[row(d), _mod_spec(False, tm, d, 0), _mod_spec(False, tm, d, 1),
                  pl.BlockSpec((1, d), lambda b, i: (0, 0)),
                  pl.BlockSpec(w_in_bf.shape, lambda b, i: (0, 0))],
        out_specs=[row(ATTN_W), row(ATTN_W), row(ATTN_W), paged, paged, row(c),
                   pl.BlockSpec((1, tm // MOBA_BLOCK, 1, ATTN_W), lambda b, i: (b, i, 0, 0))],
        out_shape=[jax.ShapeDtypeStruct((bsz, s, ATTN_W), BF16)] * 3
        + [jax.ShapeDtypeStruct((bsz, n_pages, ATTN_HEADS, PAGE_SIZE, HEAD_DIM), F32)] * 2
        + [jax.ShapeDtypeStruct((bsz, s, c), F32),
           jax.ShapeDtypeStruct((bsz, nb, 1, ATTN_W), F32)],
        compiler_params=_cparams(("arbitrary", "arbitrary")),
        name="in_proj_prompt",
    )(x, mod, mod, g_pre1, w_in_bf)


def _in_proj_sample(x, mod_rows, g_pre1, w_in_bf):
    db, t, d = x.shape
    rows = db * t
    tm = 128
    c = (w_in_bf.shape[1] - 3 * ATTN_W) // 2
    heads = pl.BlockSpec((tm // t, ATTN_HEADS, t, HEAD_DIM), lambda b, i: (i, 0, 0, 0))
    return pl.pallas_call(
        functools.partial(_in_proj_kernel, prompt=False),
        grid=(1, rows // tm),
        in_specs=[pl.BlockSpec((1, tm, d), lambda b, i: (0, i, 0)),
                  _mod_spec(True, tm, d, 0), _mod_spec(True, tm, d, 1),
                  pl.BlockSpec((1, d), lambda b, i: (0, 0)),
                  pl.BlockSpec(w_in_bf.shape, lambda b, i: (0, 0))],
        out_specs=[heads, heads, heads, pl.BlockSpec((1, tm, c), lambda b, i: (0, i, 0))],
        out_shape=[jax.ShapeDtypeStruct((db, ATTN_HEADS, t, HEAD_DIM), F32)] * 3
        + [jax.ShapeDtypeStruct((1, rows, c), F32)],
        compiler_params=_cparams(("arbitrary", "arbitrary")),
        name="in_proj_sample",
    )(x.reshape(1, rows, d), mod_rows, mod_rows, g_pre1, w_in_bf)


def _moba_prompt_kernel(q_ref, k_ref, v_ref, km_ref, o_ref, kh_scr, vh_scr, s0_scr):
    s_len = q_ref.shape[1]
    nb = s_len // MOBA_BLOCK
    lane = lax.broadcasted_iota(I32, (1, LANES), 1)
    lane_f = lane.astype(F32)
    row_i = lax.broadcasted_iota(I32, (MOBA_BLOCK, MOBA_BLOCK), 0)
    col_i = lax.broadcasted_iota(I32, (MOBA_BLOCK, MOBA_BLOCK), 1)
    causal = col_i <= row_i
    per = KV_STEP // MOBA_BLOCK
    km = km_ref[0, :, 0, :]

    heads = []
    for hh in range(2):
        in_head = (lane >= hh * HEAD_DIM) & (lane < (hh + 1) * HEAD_DIM)
        off = HEAD_DIM * (1 - hh)
        km_h = jnp.where(in_head, km, 0.0)
        parts = []
        if off:
            parts.append(jnp.zeros((off, LANES), F32))
        parts.append(km_h)
        parts.append(jnp.zeros((LANES - off - nb, LANES), F32))
        km_hi, km_lo = _split(jnp.concatenate(parts, axis=0))
        heads.append((in_head, off, km_hi, km_lo))

    def recycle(j, carry):
        r0 = pl.multiple_of(j * MOBA_BLOCK, MOBA_BLOCK)
        kj = k_ref[0, pl.ds(r0, MOBA_BLOCK), :]
        vj = v_ref[0, pl.ds(r0, MOBA_BLOCK), :]
        for hh, (in_head, off, _, _) in enumerate(heads):
            onehot = jnp.where(lane == off + j, 1.0, 0.0).astype(BF16)
            kh_scr[hh, pl.ds(r0, MOBA_BLOCK), :] = jnp.where(in_head, kj, onehot)
            vh_scr[hh, pl.ds(r0, MOBA_BLOCK), :] = jnp.where(in_head, vj, jnp.ones_like(vj))
        return carry

    lax.fori_loop(0, nb, recycle, 0)

    def q_tile(i, carry):
        q0 = pl.multiple_of(i * MOBA_BLOCK, MOBA_BLOCK)
        q = q_ref[0, pl.ds(q0, MOBA_BLOCK), :]
        kd = k_ref[0, pl.ds(q0, MOBA_BLOCK), :]
        vd = v_ref[0, pl.ds(q0, MOBA_BLOCK), :]
        q_augs, state = [], []
        for in_head, off, km_hi, km_lo in heads:
            qh = jnp.where(in_head, q, jnp.zeros_like(q))
            blk = lane - off
            in_bias = (blk >= 0) & (blk < nb)
            valid = in_bias & (blk < i)
            score = jnp.where(valid, _dot_nt(qh, km_hi) + _dot_nt(qh, km_lo), NEG_INF)
            sel = jnp.zeros(score.shape, F32)
            for _, idx in _top_n(score, lane_f, MOBA_TOP):
                sel = jnp.where((lane_f == idx) & valid, 1.0, sel)
            bias = jnp.where(in_bias & (sel == 0.0), NEG_INF, 0.0).astype(BF16)
            q_augs.append(jnp.where(in_head, q, bias))

            s = jnp.where(causal, _dot_nt(qh, kd), NEG_INF)
            m = jnp.max(s, axis=1, keepdims=True)
            p = jnp.exp(s - m)
            state.append((m, _dot(p.astype(BF16), jnp.where(in_head, vd, jnp.ones_like(vd)))))

        def scores(hh, jj):
            k0 = pl.multiple_of(jnp.minimum(jj * KV_STEP, s_len - KV_STEP), KV_STEP)
            return _dot_nt(q_augs[hh], kh_scr[hh, pl.ds(k0, KV_STEP), :])

        def absorb(hh, jj, s, m, acc):
            k0 = pl.multiple_of(jj * KV_STEP, KV_STEP)
            m_new = jnp.maximum(m, jnp.max(s, axis=1, keepdims=True))
            alpha = jnp.exp(m - m_new)
            p = jnp.exp((s - m_new).astype(BF16))
            return m_new, alpha * acc + _dot(p, vh_scr[hh, pl.ds(k0, KV_STEP), :])

        s0_scr[...] = scores(0, 0)

        def kv_step(jj, st):
            s1 = scores(1, jj)
            first = absorb(0, jj, s0_scr[...], *st[0])
            second = absorb(1, jj, s1, *st[1])
            s0_scr[...] = scores(0, jj + 1)
            return first, second

        state = lax.fori_loop(0, (i + per - 1) // per, kv_step, tuple(state))
        outs = [acc / pltpu.roll(acc, HEAD_DIM, axis=1) for _, acc in state]
        o_ref[0, pl.ds(q0, MOBA_BLOCK), :] = jnp.where(heads[0][0], outs[0], outs[1]).astype(o_ref.dtype)
        return carry

    lax.fori_loop(0, nb, q_tile, 0)


def _moba_prompt(q, k, v, km):
    bsz, s, _ = q.shape
    assert 2 * HEAD_DIM == LANES and s % KV_STEP == 0 and s // MOBA_BLOCK <= HEAD_DIM
    nb = s // MOBA_BLOCK
    spec = pl.BlockSpec((1, s, LANES), lambda b, hp: (b, 0, hp))
    return pl.pallas_call(
        _moba_prompt_kernel,
        grid=(bsz, ATTN_W // LANES),
        in_specs=[spec, spec, spec, pl.BlockSpec((1, nb, 1, LANES), lambda b, hp: (b, 0, 0, hp))],
        out_specs=spec,
        out_shape=jax.ShapeDtypeStruct((bsz, s, ATTN_W), BF16),
        scratch_shapes=[pltpu.VMEM((2, s, LANES), BF16)] * 2 + [pltpu.VMEM((MOBA_BLOCK, KV_STEP), F32)],
        compiler_params=_cparams(("arbitrary", "arbitrary")),
        name="moba_prompt",
    )(q, k, v, km)


def _moba_sample_kernel(pt_ref, q_ref, kn_ref, vn_ref, ck_ref, cv_ref, o_ref, kbuf, vbuf, sems):
    n_pages = kbuf.shape[1]
    hg = kbuf.shape[2]
    n_groups = pl.num_programs(1)
    seq = pl.program_id(0)
    grp = pl.program_id(1)
    step = seq * n_groups + grp
    total = pl.num_programs(0) * n_groups
    slot = step % 2

    def fetch(sq, gp, sl):
        def body(p, c):
            page = pt_ref[sq * n_pages + p]
            heads = pl.ds(gp * hg, hg)
            pltpu.make_async_copy(ck_ref.at[page, heads], kbuf.at[sl, p], sems.at[0, sl]).start()
            pltpu.make_async_copy(cv_ref.at[page, heads], vbuf.at[sl, p], sems.at[1, sl]).start()
            return c
        lax.fori_loop(0, n_pages, body, 0, unroll=4)

    @pl.when(step == 0)
    def _():
        fetch(seq, grp, slot)

    @pl.when(step + 1 < total)
    def _():
        nxt = step + 1
        fetch(nxt // n_groups, nxt % n_groups, 1 - slot)

    pltpu.make_async_copy(ck_ref.at[pl.ds(0, n_pages), pl.ds(0, hg)], kbuf.at[slot], sems.at[0, slot]).wait()
    pltpu.make_async_copy(cv_ref.at[pl.ds(0, n_pages), pl.ds(0, hg)], vbuf.at[slot], sems.at[1, slot]).wait()

    ppb = MOBA_BLOCK // PAGE_SIZE
    nfp = n_pages // ppb
    n_sel = min(MOBA_TOP, nfp)
    t = q_ref.shape[2]
    rows = hg * t
    width = hg * HEAD_DIM
    blk_id = lax.broadcasted_iota(I32, (nfp, 1, 1), 0).astype(F32)
    page_blk = (lax.broadcasted_iota(I32, (n_pages, 1, 1), 0) // ppb).astype(F32)
    row_head = lax.broadcasted_iota(I32, (rows, 1), 0) // t
    own = row_head == lax.broadcasted_iota(I32, (1, width), 1) // HEAD_DIM
    tile = (lax.broadcasted_iota(I32, (HEAD_DIM, width), 1) % HEAD_DIM
            == lax.broadcasted_iota(I32, (HEAD_DIM, width), 0)).astype(BF16)
    r_i = lax.broadcasted_iota(I32, (rows, rows), 0)
    c_i = lax.broadcasted_iota(I32, (rows, rows), 1)
    causal = (r_i // t == c_i // t) & (c_i % t <= r_i % t)
    batch_nn = (((2,), (1,)), ((0,), (0,)))
    batch_nt = (((2,), (2,)), ((0,), (0,)))

    def block_diag(x):
        spread = _dot(x.reshape(rows, HEAD_DIM).astype(BF16), tile)
        return jnp.where(own, spread, 0.0).astype(BF16)

    q_bd = block_diag(q_ref[0])
    kn_bd = block_diag(kn_ref[0])
    vnb = vn_ref[0].reshape(rows, HEAD_DIM).astype(BF16)
    kt = kbuf[slot].reshape(n_pages, width, PAGE_SIZE).astype(BF16)
    vt = vbuf[slot].reshape(n_pages, width, PAGE_SIZE).astype(BF16)
    s = lax.dot_general(jnp.broadcast_to(q_bd[None], (n_pages, rows, width)), kt, batch_nn,
                        preferred_element_type=F32)

    score = jnp.sum(jnp.sum(s.reshape(nfp, ppb, rows, PAGE_SIZE), axis=1), axis=2, keepdims=True)
    keep = None
    for _ in range(n_sel):
        best = jnp.max(score, axis=0, keepdims=True)
        idx = jnp.min(jnp.where(score == best, blk_id, 1e9), axis=0, keepdims=True)
        score = jnp.where(blk_id == idx, REMOVED, score)
        hit = page_blk == idx
        keep = hit if keep is None else keep | hit

    s = jnp.where(keep, s, NEG_INF)
    so = jnp.where(causal, _dot_nt(q_bd, kn_bd), NEG_INF)
    m = jnp.maximum(jnp.max(jnp.max(s, axis=0), axis=1, keepdims=True), jnp.max(so, axis=1, keepdims=True))
    p = jnp.exp(s - m[None])
    po = jnp.exp(so - m)
    denom = jnp.sum(jnp.sum(p, axis=0), axis=1, keepdims=True) + jnp.sum(po, axis=1, keepdims=True)
    pv = jnp.sum(lax.dot_general(p.astype(BF16), vt, batch_nt, preferred_element_type=F32), axis=0)
    past = jnp.zeros((rows, HEAD_DIM), F32)
    for hh in range(hg):
        past = past + jnp.where(row_head == hh, pv[:, hh * HEAD_DIM:(hh + 1) * HEAD_DIM], 0.0)
    o_ref[0] = ((past + _dot(po.astype(BF16), vnb)) / denom).reshape(hg, t, HEAD_DIM)


def _moba_sample(q, kn, vn, cache_kt, cache_vt, page_table):
    db, nh, t, _ = q.shape
    n_pages = page_table.shape[1]
    hg = SAMPLE_HEADS
    assert (n_pages * PAGE_SIZE) % MOBA_BLOCK == 0 and n_pages * PAGE_SIZE >= MOBA_BLOCK and nh % hg == 0
    new = pl.BlockSpec((1, hg, t, HEAD_DIM), lambda s, g, pt: (s, g, 0, 0))
    return pl.pallas_call(
        _moba_sample_kernel,
        grid_spec=pltpu.PrefetchScalarGridSpec(
            num_scalar_prefetch=1,
            grid=(db, nh // hg),
            in_specs=[new, new, new, pl.BlockSpec(memory_space=pl.ANY), pl.BlockSpec(memory_space=pl.ANY)],
            out_specs=new,
            scratch_shapes=[pltpu.VMEM((2, n_pages, hg, HEAD_DIM, PAGE_SIZE), F32),
                            pltpu.VMEM((2, n_pages, hg, HEAD_DIM, PAGE_SIZE), F32),
                            pltpu.SemaphoreType.DMA((2, 2))]),
        out_shape=jax.ShapeDtypeStruct((db, nh, t, HEAD_DIM), F32),
        compiler_params=_cparams(("arbitrary", "arbitrary")),
        name="moba_sample",
    )(page_table.reshape(-1), q, kn, vn, cache_kt, cache_vt)


def _conv_tail(y, lng_ref, lnb_ref):
    mu = jnp.mean(y, axis=-1, keepdims=True)
    yc = y - mu
    var = jnp.mean(yc * yc, axis=-1, keepdims=True)
    z = yc * lax.rsqrt(var + LN_EPS) * lng_ref[...] + lnb_ref[...]
    return z * jax.nn.sigmoid(z)


def _post_mix_tail(attn, conv, x, gt1, sh2, sc2, wo_ref, bo_ref, gp1_ref, gp2_ref, rw_ref, rb_ref,
                   cnt_ref, x1_ref, h2_ref, meta_ref, gate_ref):
    tm = x.shape[0]
    half = attn.shape[1]
    mixed = _dot(attn, wo_ref[:half, :]) + _dot(conv.astype(BF16), wo_ref[half:, :]) + bo_ref[...]
    x1 = x + gt1 * _rms(mixed, gp1_ref[...])
    h2 = _rms(x1, gp2_ref[...]) * (1.0 + sc2) + sh2
    x1_ref[...] = x1.reshape(x1_ref.shape)
    _store_rows_as_tiles(h2_ref, h2)

    lane = lax.broadcasted_iota(I32, (1, LANES), 1)
    lane_f = lane.astype(F32)
    h_hi, h_lo = _split(h2)
    w_hi, w_lo = _split(rw_ref[...])
    logits = _dot(h_hi, w_hi) + _dot(h_hi, w_lo) + _dot(h_lo, w_hi) + rb_ref[...]
    logits = jnp.where(lane < N_EXPERTS, logits, REMOVED)
    picks = _top_n(logits, lane_f, TOP_K)
    exps = [jnp.exp(val - picks[0][0]) for val, _ in picks]
    total = exps[0]
    for e in exps[1:]:
        total = total + e
    sel = jnp.zeros((tm, LANES), F32)
    for _, idx in picks:
        sel = jnp.where(lane_f == idx, 1.0, sel)

    before = lax.broadcasted_iota(I32, (tm, tm), 1) < lax.broadcasted_iota(I32, (tm, tm), 0)
    rank_all = cnt_ref[...] + _dot(jnp.where(before, 1.0, 0.0).astype(BF16), sel.astype(BF16))
    cnt_ref[...] = cnt_ref[...] + jnp.sum(sel, axis=0, keepdims=True)

    meta = jnp.zeros((tm, LANES), F32)
    gates = jnp.zeros((tm, LANES), F32)
    for kk, (_, idx) in enumerate(picks):
        rank = jnp.sum(jnp.where(lane_f == idx, rank_all, 0.0), axis=1, keepdims=True)
        meta = jnp.where(lane == kk, idx, meta)
        meta = jnp.where(lane == TOP_K + kk, rank, meta)
        gates = jnp.where(lane == kk, exps[kk] / total, gates)
    meta_ref[...] = meta.astype(I32).reshape(meta_ref.shape)
    gate_ref[...] = gates.reshape(gate_ref.shape)


def _post_mix_prompt_kernel(attn_ref, u_ref, halo_ref, x_ref, gt1_ref, sh2_ref, sc2_ref, cw_ref, cb_ref,
                            lng_ref, lnb_ref, wo_ref, bo_ref, gp1_ref, gp2_ref, rw_ref, rb_ref, cnt0_ref,
                            x1_ref, h2_ref, meta_ref, gate_ref, cnt_ref, ext):
    b = pl.program_id(0)
    i = pl.program_id(1)
    tm = u_ref.shape[1]

    @pl.when((b == 0) & (i == 0))
    def _():
        cnt_ref[...] = cnt0_ref[...]

    ext[0:HALO, :] = jnp.where(i > 0, halo_ref[0], 0.0)
    ext[HALO:HALO + tm, :] = u_ref[0]
    first = HALO - (CONV_TAPS - 1)
    y = cb_ref[...] + cw_ref[0:1, :] * ext[first:first + tm, :]
    for j in range(1, CONV_TAPS):
        y = y + cw_ref[j:j + 1, :] * ext[first + j:first + j + tm, :]
    conv = _conv_tail(y, lng_ref, lnb_ref)
    _post_mix_tail(attn_ref[0], conv, x_ref[0], gt1_ref[0], sh2_ref[0], sc2_ref[0], wo_ref, bo_ref,
                   gp1_ref, gp2_ref, rw_ref, rb_ref, cnt_ref, x1_ref, h2_ref, meta_ref, gate_ref)


def _post_mix_sample_kernel(attn_ref, u_ref, st_ref, x_ref, gt1_ref, sh2_ref, sc2_ref, cw_ref, cb_ref,
                            lng_ref, lnb_ref, wo_ref, bo_ref, gp1_ref, gp2_ref, rw_ref, rb_ref, cnt0_ref,
                            x1_ref, h2_ref, meta_ref, gate_ref, cnt_ref, st_out_ref, ext):
    i = pl.program_id(1)
    nseq, keep, c = st_ref.shape
    t = u_ref.shape[1] // nseq

    @pl.when(i == 0)
    def _():
        cnt_ref[...] = cnt0_ref[...]

    ext[:, 0:keep, :] = st_ref[...]
    ext[:, keep:keep + t, :] = u_ref[0].reshape(nseq, t, c)
    y = cb_ref[...] + cw_ref[0:1, :] * ext[:, 0:t, :]
    for j in range(1, CONV_TAPS):
        y = y + cw_ref[j:j + 1, :] * ext[:, j:j + t, :]
    st_out_ref[...] = ext[:, t:t + keep, :]
    conv = _conv_tail(y.reshape(nseq * t, c), lng_ref, lnb_ref)
    _post_mix_tail(attn_ref[0], conv, x_ref[0], gt1_ref[0], sh2_ref[0], sc2_ref[0], wo_ref, bo_ref,
                   gp1_ref, gp2_ref, rw_ref, rb_ref, cnt_ref, x1_ref, h2_ref, meta_ref, gate_ref)


def _post_mix(prompt, attn, u, prefix, x, mod, cnt0, cw, cb, lng, lnb, wo_bf, bo, gp1, gp2, rw, rb):
    bsz, s, d = x.shape
    c = u.shape[2]
    tm = 256 if prompt else 128
    row = lambda w: pl.BlockSpec((1, tm, w), lambda b, i: (b, i, 0))
    full = lambda a: pl.BlockSpec(a.shape, lambda b, i: (0,) * a.ndim)
    weights = (cw, cb, lng, lnb, wo_bf, bo, gp1, gp2, rw, rb, cnt0)
    mods = [_mod_spec(not prompt, tm, d, ch) for ch in (2, 3, 4)]
    assert d == SUBLANES * LANES, "a token row of the MoE input must fill exactly one (8,128) tile"
    per_b = s // tm
    tiles = pl.BlockSpec((tm * SUBLANES, LANES), lambda b, i: (b * per_b + i, 0))
    out_specs = [row(d), tiles, row(LANES), row(LANES), pl.BlockSpec((1, LANES), lambda b, i: (0, 0))]
    out_shape = [jax.ShapeDtypeStruct((bsz, s, d), F32), jax.ShapeDtypeStruct((bsz * s * SUBLANES, LANES), F32),
                 jax.ShapeDtypeStruct((bsz, s, LANES), I32), jax.ShapeDtypeStruct((bsz, s, LANES), F32),
                 jax.ShapeDtypeStruct((1, LANES), F32)]
    if prompt:
        kern = _post_mix_prompt_kernel
        per = tm // HALO
        side = pl.BlockSpec((1, HALO, c), lambda b, i: (b, jnp.maximum(i * per - 1, 0), 0))
        side_arg = u
        scratch = [pltpu.VMEM((HALO + tm, c), F32)]
    else:
        kern = _post_mix_sample_kernel
        nseq, keep = prefix.shape[0], prefix.shape[1]
        t = s // nseq
        per = tm // t
        side = pl.BlockSpec((per, keep, c), lambda b, i: (i, 0, 0))
        side_arg = prefix
        out_specs.append(pl.BlockSpec((per, keep, c), lambda b, i: (i, 0, 0)))
        out_shape.append(jax.ShapeDtypeStruct(prefix.shape, F32))
        scratch = [pltpu.VMEM((per, -(-(keep + t) // SUBLANES) * SUBLANES, c), F32)]
    return pl.pallas_call(
        kern,
        grid=(bsz, s // tm),
        in_specs=[row(ATTN_W), row(c), side, row(d)] + mods + [full(a) for a in weights],
        out_specs=out_specs,
        out_shape=out_shape,
        scratch_shapes=scratch,
        compiler_params=_cparams(("arbitrary", "arbitrary")),
        name="post_mix_prompt" if prompt else "post_mix_sample",
    )(attn, u, side_arg, x, mod, mod, mod, *weights)


def _pos_spec(index):
    return pl.BlockSpec((1, 1, TOKEN_TILE * TOP_K), index, memory_space=pltpu.SMEM)


def _tile_rows(ref, row):
    return ref.at[pl.ds(pl.multiple_of(row * SUBLANES, SUBLANES), SUBLANES)]


def _moe_scatter_kernel(cnt_ref, pst_ref, pos_ref, hp_ref, hs_ref, xr_ref, sem, *, tiles_p, n_pad):
    i = pl.program_id(0)
    n_rows = xr_ref.shape[0] // SUBLANES

    def wait_tiles(n):
        for _ in range(n):
            pltpu.make_async_copy(hp_ref, xr_ref.at[pl.ds(0, TOKEN_TILE * SUBLANES)], sem).wait()

    @pl.when(i == 0)
    def _():
        def fill(lo, hi):
            def body(r, c):
                pltpu.make_async_copy(_tile_rows(hp_ref, 0), _tile_rows(xr_ref, r), sem).start()
                return c
            lax.fori_loop(lo, hi, body, 0)

        end = 0
        for e in range(N_EXPERTS):
            used = pst_ref[e] + cnt_ref[e]
            end = pst_ref[e] + (cnt_ref[e] + ROW_BLOCK - 1) // ROW_BLOCK * ROW_BLOCK
            fill(used, end)
        fill(end, n_rows)
        wait_tiles(n_pad // TOKEN_TILE)

    def scatter_tile(src_ref):
        def body(r, c):
            for kk in range(TOP_K):
                dst = pos_ref[0, 0, r * TOP_K + kk]
                pltpu.make_async_copy(_tile_rows(src_ref, r), _tile_rows(xr_ref, dst), sem).start(
                    priority=kk % DMA_PRIORITIES)
            return c
        lax.fori_loop(0, TOKEN_TILE, body, 0, unroll=4)
        wait_tiles(TOP_K)

    @pl.when(i < tiles_p)
    def _():
        scatter_tile(hp_ref)

    @pl.when(i >= tiles_p)
    def _():
        scatter_tile(hs_ref)


def _moe_scatter(counts, pstart, pos_tiles, h_p, h_s, n_rows):
    rows_t = TOKEN_TILE * SUBLANES
    tiles_p = h_p.shape[0] // rows_t
    n_tok = (h_p.shape[0] + h_s.shape[0]) // SUBLANES
    n_pad = n_rows - n_tok * TOP_K
    assert h_p.shape[0] % rows_t == 0 and h_s.shape[0] % rows_t == 0 and n_pad % TOKEN_TILE == 0
    return pl.pallas_call(
        functools.partial(_moe_scatter_kernel, tiles_p=tiles_p, n_pad=n_pad),
        grid_spec=pltpu.PrefetchScalarGridSpec(
            num_scalar_prefetch=2,
            grid=(pos_tiles.shape[0],),
            in_specs=[_pos_spec(lambda i, cnt, pst: (i, 0, 0)),
                      pl.BlockSpec((rows_t, LANES), lambda i, cnt, pst: (jnp.minimum(i, tiles_p - 1), 0)),
                      pl.BlockSpec((rows_t, LANES), lambda i, cnt, pst: (jnp.maximum(i - tiles_p, 0), 0))],
            out_specs=pl.BlockSpec(memory_space=pl.ANY),
            scratch_shapes=[pltpu.SemaphoreType.DMA(())]),
        out_shape=jax.ShapeDtypeStruct((n_rows * SUBLANES, LANES), F32),
        compiler_params=_cparams(("arbitrary",)),
        name="moe_scatter",
    )(counts, pstart, pos_tiles, h_p, h_s)


def _moe_expert_kernel(be_ref, na_ref, x_ref, wgu_ref, bgu_ref, wd_ref, bd_ref, y_ref, wgu_bf, wd_bf):
    b = pl.program_id(0)
    active = b < na_ref[0]
    prev = be_ref[jnp.maximum(b - 1, 0)]

    @pl.when(active & ((b == 0) | (be_ref[b] != prev)))
    def _():
        wgu_bf[...] = wgu_ref[0].astype(BF16)
        wd_bf[...] = wd_ref[0].astype(BF16)

    @pl.when(active)
    def _():
        de = wd_bf.shape[0]
        x = _load_rows_from_tiles(x_ref, ROW_BLOCK)
        gu = _dot(x.astype(BF16), wgu_bf[...]) + bgu_ref[0]
        glu = jnp.minimum(gu[:, :de], SWIGLU_LIMIT)
        lin = jnp.clip(gu[:, de:], -SWIGLU_LIMIT, SWIGLU_LIMIT)
        act = glu * jax.nn.sigmoid(SWIGLU_ALPHA * glu) * (lin + 1.0)
        _store_rows_as_tiles(y_ref, _dot(act.astype(BF16), wd_bf[...]) + bd_ref[0])

    @pl.when(jnp.logical_not(active))
    def _():
        y_ref[...] = jnp.zeros(y_ref.shape, y_ref.dtype)


def _moe_expert(block_e, n_active, xr, wgu, bgu, wd, bd):
    ne, d, two_de = wgu.shape
    de = wd.shape[1]
    nblk = xr.shape[0] // (ROW_BLOCK * SUBLANES)
    tiles = lambda index: pl.BlockSpec((ROW_BLOCK * SUBLANES, LANES), index)
    last = lambda b, na: jnp.minimum(b, na[0] - 1)
    return pl.pallas_call(
        _moe_expert_kernel,
        grid_spec=pltpu.PrefetchScalarGridSpec(
            num_scalar_prefetch=2,
            grid=(nblk,),
            in_specs=[tiles(lambda b, be, na: (last(b, na), 0)),
                      pl.BlockSpec((1, d, two_de), lambda b, be, na: (be[last(b, na)], 0, 0)),
                      pl.BlockSpec((1, 1, two_de), lambda b, be, na: (be[last(b, na)], 0, 0)),
                      pl.BlockSpec((1, de, d), lambda b, be, na: (be[last(b, na)], 0, 0)),
                      pl.BlockSpec((1, 1, d), lambda b, be, na: (be[last(b, na)], 0, 0))],
            out_specs=tiles(lambda b, be, na: (b, 0)),
            scratch_shapes=[pltpu.VMEM((d, two_de), BF16), pltpu.VMEM((de, d), BF16)]),
        out_shape=jax.ShapeDtypeStruct(xr.shape, F32),
        compiler_params=_cparams(("arbitrary",)),
        name="moe_expert",
    )(block_e, n_active, xr, wgu, bgu.reshape(ne, 1, two_de), wd, bd.reshape(ne, 1, d))


def _moe_combine_kernel(pos_ref, npos_ref, yr_ref, gate_ref, x1_ref, gt2_ref, g_ref, o_ref, buf, sems):
    step = pl.program_id(0) * pl.num_programs(1) + pl.program_id(1)
    total = pl.num_programs(0) * pl.num_programs(1)
    slot = step % 2
    tm = buf.shape[2] // SUBLANES

    def gather(m_ref, sl):
        def body(r, c):
            for kk in range(TOP_K):
                src = m_ref[0, 0, r * TOP_K + kk]
                pltpu.make_async_copy(_tile_rows(yr_ref, src), _tile_rows(buf.at[sl, kk], r), sems.at[sl]).start(
                    priority=kk % DMA_PRIORITIES)
            return c
        lax.fori_loop(0, tm, body, 0, unroll=4)

    @pl.when(step == 0)
    def _():
        gather(pos_ref, slot)

    @pl.when(step + 1 < total)
    def _():
        gather(npos_ref, 1 - slot)

    for kk in range(TOP_K):
        pltpu.make_async_copy(yr_ref.at[pl.ds(0, tm * SUBLANES)], buf.at[slot, kk], sems.at[slot]).wait()
    gates = gate_ref[0]
    f = gates[:, 0:1] * _load_rows_from_tiles(buf, tm, (slot, 0))
    for kk in range(1, TOP_K):
        f = f + gates[:, kk:kk + 1] * _load_rows_from_tiles(buf, tm, (slot, kk))
    o_ref[0] = x1_ref[0] + gt2_ref[0] * _rms(f, g_ref[...])


def _moe_combine(prompt, pos_tiles, tile0, yr, gates, x1, mod, g_post2):
    bsz, s, d = x1.shape
    tm = TOKEN_TILE
    per_b = s // tm
    last = tile0 + bsz * per_b - 1
    row = lambda w: pl.BlockSpec((1, tm, w), lambda b, i: (b, i, 0))
    if prompt:
        gt2 = pl.BlockSpec((1, 1, d), lambda b, i: (b, 0, 5))
    else:
        gt2 = pl.BlockSpec((1, tm, d), lambda b, i: (0, i, 5))
    return pl.pallas_call(
        _moe_combine_kernel,
        grid=(bsz, per_b),
        in_specs=[_pos_spec(lambda b, i: (tile0 + b * per_b + i, 0, 0)),
                  _pos_spec(lambda b, i: (jnp.minimum(tile0 + b * per_b + i + 1, last), 0, 0)),
                  pl.BlockSpec(memory_space=pl.ANY),
                  row(LANES), row(d), gt2,
                  pl.BlockSpec((1, d), lambda b, i: (0, 0))],
        out_specs=row(d),
        scratch_shapes=[pltpu.VMEM((2, TOP_K, tm * SUBLANES, LANES), F32), pltpu.SemaphoreType.DMA((2,))],
        out_shape=jax.ShapeDtypeStruct((bsz, s, d), F32),
        compiler_params=_cparams(("arbitrary", "arbitrary")),
        name="moe_combine_prompt" if prompt else "moe_combine_sample",
    )(pos_tiles, pos_tiles, yr, gates, x1, mod, g_post2)


def kernel(x_prompt, x_sample, cache_k, cache_v, state_conv, page_table, c_prompt, c_sample, w_mod, b_mod,
           g_pre1, g_post1, g_pre2, g_post2, w_in, conv_w, conv_b, conv_ln_g, conv_ln_b, w_out, b_out,
           router_w, router_b, w_gate_up, b_gate_up, w_down, b_down):
    assert w_mod.shape[0] == 1, "single-layer trunk"
    bsz, s, d = x_prompt.shape
    db, t, _ = x_sample.shape
    c = conv_w.shape[2]
    keep = CONV_TAPS - 1
    rows_s = db * t

    mod = _mod(jnp.concatenate([c_prompt, c_sample], axis=0), w_mod[0], b_mod[0])
    mod_p = mod[:bsz].reshape(bsz, 1, 6 * d)
    mod_s = jnp.repeat(mod[bsz:], t, axis=0).reshape(1, rows_s, 6 * d)

    w_in_bf = w_in[0].astype(BF16)
    wo_bf = w_out[0].astype(BF16)
    cw = jnp.concatenate([conv_w[0], jnp.zeros((HALO - CONV_TAPS, c), F32)], axis=0)
    rw = jnp.concatenate([router_w[0], jnp.zeros((d, LANES - N_EXPERTS), F32)], axis=1)
    rb = jnp.concatenate([router_b[0], jnp.zeros((LANES - N_EXPERTS,), F32)]).reshape(1, LANES)
    shared = (cw, conv_b, conv_ln_g, conv_ln_b, wo_bf, b_out, g_post1, g_pre2, rw, rb)

    q_p, kb_p, vb_p, k_pages, v_pages, u_p, km_p = _in_proj_prompt(x_prompt, mod_p, g_pre1, w_in_bf)
    q_s, k_s, v_s, u_s = _in_proj_sample(x_sample, mod_s, g_pre1, w_in_bf)

    attn_p = _moba_prompt(q_p, kb_p, vb_p, km_p)
    attn_s = _moba_sample(q_s, k_s, v_s, jnp.swapaxes(cache_k[0], 2, 3), jnp.swapaxes(cache_v[0], 2, 3), page_table)
    attn_s = attn_s.transpose(0, 2, 1, 3).reshape(1, rows_s, ATTN_W).astype(BF16)

    cnt0 = jnp.zeros((1, LANES), F32)
    x1_p, h2_p, meta_p, gate_p, cnt_p = _post_mix(True, attn_p, u_p, None, x_prompt, mod_p, cnt0, *shared)
    x1_s, h2_s, meta_s, gate_s, cnt, conv_s = _post_mix(False, attn_s, u_s, state_conv[0],
                                                        x_sample.reshape(1, rows_s, d), mod_s, cnt_p, *shared)

    n_tok = bsz * s + rows_s
    counts = cnt[0, :N_EXPERTS].astype(I32)
    padded = (counts + ROW_BLOCK - 1) // ROW_BLOCK * ROW_BLOCK
    pend = jnp.cumsum(padded)
    pstart = pend - padded
    nblk = -(-(n_tok * TOP_K) // ROW_BLOCK) + N_EXPERTS
    starts = jnp.arange(nblk, dtype=I32) * ROW_BLOCK
    block_e = jnp.minimum(jnp.sum((pend[None, :] <= starts[:, None]).astype(I32), axis=1), N_EXPERTS - 1)
    n_active = (pend[-1:] // ROW_BLOCK).astype(I32)
    meta = jnp.concatenate([meta_p.reshape(bsz * s, LANES)[:, :2 * TOP_K],
                            meta_s.reshape(rows_s, LANES)[:, :2 * TOP_K]], axis=0)
    pos = pstart[meta[:, :TOP_K]] + meta[:, TOP_K:]
    pos_tiles = pos.reshape(n_tok // TOKEN_TILE, 1, TOKEN_TILE * TOP_K)

    xr = _moe_scatter(counts, pstart, pos_tiles, h2_p, h2_s, nblk * ROW_BLOCK)
    yr = _moe_expert(block_e, n_active, xr, w_gate_up[0], b_gate_up[0], w_down[0], b_down[0])
    y_p = _moe_combine(True, pos_tiles, 0, yr, gate_p, x1_p, mod_p, g_post2)
    y_s = _moe_combine(False, pos_tiles, bsz * s // TOKEN_TILE, yr, gate_s, x1_s, mod_s, g_post2)

    conv_p = u_p[:, s - keep:, :]
    return (y_p, y_s.reshape(db, t, d), k_pages[None], v_pages[None], conv_p[None],
            k_s[None], v_s[None], conv_s[None])
```
